```python
import jax, jax.numpy as jnp
from jax import lax
import numpy as np

D_MODEL = 4096
BATCH = 2
SEQ = 8192
DEPTH = 1
DEC_BATCH = 8
DEC_SEQ = 32
PAST_LEN = 4096

CHUNK = 64
NH_A = 8
DK_A = D_MODEL // 16
DV_A = D_MODEL // 8
QK_A = NH_A * DK_A
V_A = NH_A * DV_A
CONV_W = 4
NH_B = 32
DH_B = D_MODEL // NH_B
W_B = NH_B * DH_B
N_EXPERTS = 32
TOP_K = 4
D_EXPERT = D_MODEL
SWIGLU_LIMIT = 7.0
SWIGLU_ALPHA = 1.702
MOE_BLOCK = 256
EPS = 1e-6
SPLIT_SIZES = (2 * QK_A, V_A, V_A, NH_A, NH_A, W_B, W_B, W_B, W_B, D_MODEL, D_MODEL)
D_IN = sum(SPLIT_SIZES)

kernel_name = 'hybrid_mlstm_hgrn2_moe_stream_step'


def _split_offsets():
    return [int(v) for v in np.cumsum(SPLIT_SIZES)[:-1]]


def _rmsnorm(x, g):
    xf = x.astype(jnp.float32)
    y = xf * lax.rsqrt(jnp.mean(xf * xf, axis=-1, keepdims=True) + EPS)
    return (y * g.astype(jnp.float32)).astype(x.dtype)


def _head_rmsnorm(h, g):
    B, T, H, d = h.shape
    y = h * lax.rsqrt(jnp.mean(h * h, axis=-1, keepdims=True) + EPS)
    return y.reshape(B, T, H * d) * g.astype(jnp.float32)


def _causal_conv(x, buf, w, b):
    T = x.shape[1]
    xp = jnp.concatenate([buf, x], axis=1)
    y = b + xp[:, 0:T] * w[0]
    for j in range(1, CONV_W):
        y = y + xp[:, j:j + T] * w[j]
    return y, xp[:, xp.shape[1] - (CONV_W - 1):]


def _chunked(step, carry, xs):
    B, T = xs[0].shape[0], xs[0].shape[1]
    L = min(T, CHUNK)
    nc = T // L
    blocks = tuple(jnp.moveaxis(a.reshape(B, nc, L, *a.shape[2:]), 1, 0) for a in xs)
    carry, ys = lax.scan(step, carry, blocks)
    ys = jnp.moveaxis(ys, 0, 1)
    return carry, ys.reshape(B, T, *ys.shape[3:])


def _mlstm_block(carry, inp):
    C, n, m = carry
    q, k, v, ig, lf = inp
    L = q.shape[1]
    F = jnp.cumsum(lf, axis=1)
    g = F + m[:, None, :]
    causal = jnp.tril(jnp.ones((L, L), bool))[None, :, :, None]
    D = jnp.where(causal, F[:, :, None, :] - F[:, None, :, :] + ig[:, None, :, :], -jnp.inf)
    m_t = jnp.maximum(g, jnp.max(D, axis=2))
    w = jnp.exp(D - m_t[:, :, None, :])
    a = jnp.exp(g - m_t)
    s = jnp.einsum('bthd,bshd->btsh', q, k) * w
    num = a[..., None] * jnp.einsum('bthd,bhde->bthe', q, C) + jnp.einsum('btsh,bshe->bthe', s, v)
    den = a * jnp.einsum('bthd,bhd->bth', q, n) + jnp.sum(s, axis=2)
    h = num / jnp.maximum(jnp.abs(den), jnp.exp(-m_t))[..., None]
    aL, wL = a[:, -1], w[:, -1]
    C_new = aL[:, :, None, None] * C + jnp.einsum('bsh,bshd,bshe->bhde', wL, k, v)
    n_new = aL[..., None] * n + jnp.einsum('bsh,bshd->bhd', wL, k)
    return (C_new, n_new, m_t[:, -1]), h


def _hgrn2_block(S, inp):
    q, k, lf, i = inp
    L = q.shape[1]
    b = jnp.cumsum(lf, axis=1)
    causal = jnp.tril(jnp.ones((L, L), bool))[None, :, :, None, None]
    decay = jnp.exp(jnp.where(causal, b[:, :, None] - b[:, None], -jnp.inf))
    A = jnp.einsum('bthc,bshc,btshc->btsh', q, k, decay)
    o = jnp.einsum('bthc,bhce->bthe', q * jnp.exp(b), S) + jnp.einsum('btsh,bshe->bthe', A, i)
    bL = b[:, -1]
    S_new = jnp.exp(bL)[..., None] * S + jnp.einsum('bshc,bshe->bhce', k * jnp.exp(bL[:, None] - b), i)
    return S_new, o


def _token_mixers(u, l, conv_buf, C, n, m, S, w_in, b_in, conv_w, conv_b, gain_a, gain_b,
                  lb_logits, w_branch_a, w_branch_b, w_out):
    f32 = jnp.float32
    B, T, _ = u.shape
    dt = u.dtype
    proj = (u @ w_in[l] + b_in[l]).astype(f32)
    qk, v, o_a, ig, fg, hf, hi, hq, hg, ga, gb = jnp.split(proj, _split_offsets(), axis=-1)
    qk_c, conv_new = _causal_conv(qk, conv_buf.astype(f32), conv_w[l].astype(f32), conv_b[l].astype(f32))
    qk_c = jax.nn.silu(qk_c)
    q_a = qk_c[..., :QK_A].reshape(B, T, NH_A, DK_A)
    k_a = qk_c[..., QK_A:].reshape(B, T, NH_A, DK_A) * (DK_A ** -0.5)
    v_a = v.reshape(B, T, NH_A, DV_A)
    (C, n, m), h_a = _chunked(_mlstm_block, (C.astype(f32), n.astype(f32), m.astype(f32)),
                              (q_a, k_a, v_a, ig, jax.nn.log_sigmoid(fg)))
    y_a = _head_rmsnorm(h_a, gain_a[l]) * jax.nn.sigmoid(o_a)
    lb = jnp.cumsum(jax.nn.softmax(lb_logits.astype(f32), axis=0), axis=0)[l]
    f_b = lb + (1.0 - lb) * jax.nn.sigmoid(hf)
    shp = (B, T, NH_B, DH_B)
    S, o_b = _chunked(_hgrn2_block, S.astype(f32),
                      (jax.nn.silu(hq).reshape(shp), (1.0 - f_b).reshape(shp),
                       jnp.log(f_b).reshape(shp), hi.reshape(shp)))
    y_b = _head_rmsnorm(o_b, gain_b[l]) * jax.nn.silu(hg)
    z = (jax.nn.sigmoid(ga) * (y_a.astype(dt) @ w_branch_a[l]).astype(f32)
         + jax.nn.sigmoid(gb) * (y_b.astype(dt) @ w_branch_b[l]).astype(f32))
    return z.astype(dt) @ w_out[l], conv_new, C, n, m, S


def _moe(x, l, w_router, b_router, w_gate_up, b_gate_up, w_down, b_down):
    f32 = jnp.float32
    B, T, D = x.shape
    xt = x.reshape(-1, D)
    NT = xt.shape[0]
    n_assign = NT * TOP_K
    logits = xt.astype(f32) @ w_router[l].astype(f32) + b_router[l].astype(f32)
    top_v, top_e = lax.top_k(logits, TOP_K)
    gate = jax.nn.softmax(top_v, axis=-1)
    flat_e = top_e.reshape(-1)
    flat_g = gate.reshape(-1)
    flat_tok = jnp.arange(n_assign, dtype=jnp.int32) // TOP_K
    order = jnp.argsort(flat_e)
    e_sorted = flat_e[order]
    counts = jnp.bincount(flat_e, length=N_EXPERTS)
    padded = (counts + MOE_BLOCK - 1) // MOE_BLOCK * MOE_BLOCK
    pad_end = jnp.cumsum(padded)
    pad_start = pad_end - padded
    sort_start = jnp.cumsum(counts) - counts
    dest = pad_start[e_sorted] + jnp.arange(n_assign) - sort_start[e_sorted]
    n_blocks = -(-n_assign // MOE_BLOCK) + N_EXPERTS
    n_slots = n_blocks * MOE_BLOCK
    slot_tok = jnp.full((n_slots,), NT, jnp.int32).at[dest].set(flat_tok[order])
    slot_g = jnp.zeros((n_slots,), f32).at[dest].set(flat_g[order])
    block_e = jnp.minimum(jnp.searchsorted(pad_end, jnp.arange(n_blocks) * MOE_BLOCK, side='right'),
                          N_EXPERTS - 1)
    x_pad = jnp.concatenate([xt, jnp.zeros((1, D), xt.dtype)], axis=0)

    def body(acc, blk):
        tok, g, e = blk
        hu = x_pad[tok] @ w_gate_up[l, e] + b_gate_up[l, e]
        gt = jnp.minimum(hu[:, :D_EXPERT].astype(f32), SWIGLU_LIMIT)
        up = jnp.clip(hu[:, D_EXPERT:].astype(f32), -SWIGLU_LIMIT, SWIGLU_LIMIT)
        act = (up + 1.0) * gt * jax.nn.sigmoid(gt * SWIGLU_ALPHA)
        yb = act.astype(x.dtype) @ w_down[l, e] + b_down[l, e]
        return acc.at[tok].add(yb.astype(f32) * g[:, None]), None

    acc, _ = lax.scan(body, jnp.zeros((NT + 1, D), f32),
                      (slot_tok.reshape(n_blocks, MOE_BLOCK), slot_g.reshape(n_blocks, MOE_BLOCK), block_e))
    return acc[:NT].reshape(B, T, D).astype(x.dtype)


def _trunk(x, conv, C, n, m, S, g_mix, w_in, b_in, conv_w, conv_b, gain_a, gain_b, lb_logits,
           w_branch_a, w_branch_b, w_out, g_ffn, w_router, b_router, w_gate_up, b_gate_up,
           w_down, b_down, g_final):
    h = x
    outs = ([], [], [], [], [])
    for l in range(DEPTH):
        mix, cv, Cl, nl, ml, Sl = _token_mixers(_rmsnorm(h, g_mix[l]), l, conv[l], C[l], n[l], m[l], S[l],
                                                w_in, b_in, conv_w, conv_b, gain_a, gain_b, lb_logits,
                                                w_branch_a, w_branch_b, w_out)
        h = h + mix.astype(h.dtype)
        h = h + _moe(_rmsnorm(h, g_ffn[l]), l, w_router, b_router, w_gate_up, b_gate_up, w_down, b_down)
        for lst, s in zip(outs, (cv, Cl, nl, ml, Sl)):
            lst.append(s)
    states = [jnp.stack(lst) for lst in outs]
    return _rmsnorm(h, g_final), states


def setup_inputs(seed: int = 0) -> dict:
    key = jax.random.key(seed)
    ks = jax.random.split(key, 28)
    f32 = jnp.float32

    def nrm(k, shape, scale):
        return jax.random.normal(k, shape, f32) * scale

    fg_off = _split_offsets()[3]
    b_in = nrm(ks[9], (DEPTH, D_IN), 0.02).at[:, fg_off:fg_off + NH_A].add(jnp.linspace(3.0, 6.0, NH_A))
    return {
        'x_prompt': nrm(ks[0], (BATCH, SEQ, D_MODEL), 1.0),
        'x_sample': nrm(ks[1], (DEC_BATCH, DEC_SEQ, D_MODEL), 1.0),
        'state_conv': nrm(ks[2], (DEPTH, DEC_BATCH, CONV_W - 1, 2 * QK_A), 1.0),
        'state_mlstm_C': nrm(ks[3], (DEPTH, DEC_BATCH, NH_A, DK_A, DV_A), 0.1),
        'state_mlstm_n': nrm(ks[4], (DEPTH, DEC_BATCH, NH_A, DK_A), 0.5),
        'state_mlstm_m': nrm(ks[5], (DEPTH, DEC_BATCH, NH_A), 0.5),
        'state_hgrn_S': nrm(ks[6], (DEPTH, DEC_BATCH, NH_B, DH_B, DH_B), 0.5),
        'g_mix': 1.0 + nrm(ks[7], (DEPTH, D_MODEL), 0.02),
        'w_in': nrm(ks[8], (DEPTH, D_MODEL, D_IN), D_MODEL ** -0.5),
        'b_in': b_in,
        'conv_w': nrm(ks[10], (DEPTH, CONV_W, 2 * QK_A), CONV_W ** -0.5),
        'conv_b': nrm(ks[11], (DEPTH, 2 * QK_A), 0.02),
        'gain_a': 1.0 + nrm(ks[12], (DEPTH, V_A), 0.02),
        'gain_b': 1.0 + nrm(ks[13], (DEPTH, W_B), 0.02),
        'lb_logits': nrm(ks[14], (DEPTH + 1, W_B), 0.5),
        'w_branch_a': nrm(ks[15], (DEPTH, V_A, D_MODEL), V_A ** -0.5),
        'w_branch_b': nrm(ks[16], (DEPTH, W_B, D_MODEL), W_B ** -0.5),
        'w_out': nrm(ks[17], (DEPTH, D_MODEL, D_MODEL), D_MODEL ** -0.5),
        'g_ffn': 1.0 + nrm(ks[18], (DEPTH, D_MODEL), 0.02),
        'w_router': nrm(ks[19], (DEPTH, D_MODEL, N_EXPERTS), D_MODEL ** -0.5),
        'b_router': nrm(ks[20], (DEPTH, N_EXPERTS), 0.01),
        'w_gate_up': nrm(ks[21], (DEPTH, N_EXPERTS, D_MODEL, 2 * D_EXPERT), D_MODEL ** -0.5),
        'b_gate_up': nrm(ks[22], (DEPTH, N_EXPERTS, 2 * D_EXPERT), 0.01),
        'w_down': nrm(ks[23], (DEPTH, N_EXPERTS, D_EXPERT, D_MODEL), D_EXPERT ** -0.5),
        'b_down': nrm(ks[24], (DEPTH, N_EXPERTS, D_MODEL), 0.01),
        'g_final': 1.0 + nrm(ks[25], (D_MODEL,), 0.02),
    }


def reference(x_prompt, x_sample, state_conv, state_mlstm_C, state_mlstm_n, state_mlstm_m, state_hgrn_S,
              g_mix, w_in, b_in, conv_w, conv_b, gain_a, gain_b, lb_logits, w_branch_a, w_branch_b, w_out,
              g_ffn, w_router, b_router, w_gate_up, b_gate_up, w_down, b_down, g_final):
    params = (g_mix, w_in, b_in, conv_w, conv_b, gain_a, gain_b, lb_logits, w_branch_a, w_branch_b, w_out,
              g_ffn, w_router, b_router, w_gate_up, b_gate_up, w_down, b_down, g_final)
    f32 = jnp.float32
    B = x_prompt.shape[0]
    y_prompt, (conv_p, C_p, n_p, m_p, S_p) = _trunk(
        x_prompt,
        jnp.zeros((DEPTH, B, CONV_W - 1, 2 * QK_A), f32),
        jnp.zeros((DEPTH, B, NH_A, DK_A, DV_A), f32),
        jnp.zeros((DEPTH, B, NH_A, DK_A), f32),
        jnp.zeros((DEPTH, B, NH_A), f32),
        jnp.zeros((DEPTH, B, NH_B, DH_B, DH_B), f32),
        *params)
    y_sample, (conv_s, C_s, n_s, m_s, S_s) = _trunk(
        x_sample, state_conv, state_mlstm_C, state_mlstm_n, state_mlstm_m, state_hgrn_S, *params)
    return (y_prompt, y_sample,
            conv_p.astype(state_conv.dtype), C_p.astype(state_mlstm_C.dtype), n_p.astype(state_mlstm_n.dtype),
            m_p.astype(state_mlstm_m.dtype), S_p.astype(state_hgrn_S.dtype),
            conv_s.astype(state_conv.dtype), C_s.astype(state_mlstm_C.dtype), n_s.astype(state_mlstm_n.dtype),
            m_s.astype(state_mlstm_m.dtype), S_s.astype(state_hgrn_S.dtype))
```

```python
import functools
import math

import numpy as np
import jax
import jax.numpy as jnp
from jax import lax
from jax.experimental import pallas as pl
from jax.experimental.pallas import tpu as pltpu

_F32 = jnp.float32
_BF16 = jnp.bfloat16

EPS = 1e-6
TOP_K = 4
CHUNK = 64
SWIGLU_LIMIT = 7.0
SWIGLU_ALPHA = 1.702
MOE_ROWS = 256
HGRN_HEADS_PER_STEP = 8

_VMEM_LIMIT_BYTES = 56 * 1024 * 1024
_LANES = 128

_NT = (((1,), (1,)), ((), ()))
_TN = (((0,), (0,)), ((), ()))


def _params(*sem):
    return pltpu.CompilerParams(dimension_semantics=sem, vmem_limit_bytes=_VMEM_LIMIT_BYTES)


def _tile(n, pref, mult):
    best = None
    for t in range(mult, min(n, pref) + 1, mult):
        if n % t == 0:
            best = t
    return best if best is not None else n


def _sigmoid(x):
    return 1.0 / (1.0 + jnp.exp(-x))


def _dot(a, b):
    return jnp.dot(a, b, preferred_element_type=_F32)


def _rms_cast_kernel(x_ref, g_ref, o_ref):
    x = x_ref[...]
    y = x * lax.rsqrt(jnp.mean(x * x, axis=-1, keepdims=True) + EPS) * g_ref[...]
    o_ref[...] = y.astype(o_ref.dtype)


def _rms_cast(x, g, out_dtype):
    n, d = x.shape
    tm = _tile(n, 512, 8)
    return pl.pallas_call(
        _rms_cast_kernel,
        grid=(n // tm,),
        in_specs=[pl.BlockSpec((tm, d), lambda i: (i, 0)), pl.BlockSpec((1, d), lambda i: (0, 0))],
        out_specs=pl.BlockSpec((tm, d), lambda i: (i, 0)),
        out_shape=jax.ShapeDtypeStruct((n, d), out_dtype),
        compiler_params=_params("parallel"),
        name="rms_cast",
    )(x, g.reshape(1, d))


def _mm_kernel(x_ref, w_ref, b_ref, o_ref):
    o_ref[...] = (_dot(x_ref[...], w_ref[...]) + b_ref[...]).astype(o_ref.dtype)


def _mm_bias(x, w, b, out_dtype, tm_pref=640, tn_pref=1024):
    m, k = x.shape
    n = w.shape[1]
    tm = _tile(m, tm_pref, 8)
    tn = _tile(n, tn_pref, _LANES)
    return pl.pallas_call(
        _mm_kernel,
        grid=(m // tm, n // tn),
        in_specs=[
            pl.BlockSpec((tm, k), lambda i, j: (i, 0)),
            pl.BlockSpec((k, tn), lambda i, j: (0, j)),
            pl.BlockSpec((1, tn), lambda i, j: (0, j)),
        ],
        out_specs=pl.BlockSpec((tm, tn), lambda i, j: (i, j)),
        out_shape=jax.ShapeDtypeStruct((m, n), out_dtype),
        compiler_params=_params("parallel", "parallel"),
        name="mm_bias",
    )(x, w, b.reshape(1, n))


def _merge_kernel(ya_ref, yb_ref, wa_ref, wb_ref, ga_ref, gb_ref, o_ref):
    a = _dot(ya_ref[...], wa_ref[...])
    b = _dot(yb_ref[...], wb_ref[...])
    z = _sigmoid(ga_ref[...]) * a + _sigmoid(gb_ref[...]) * b
    o_ref[...] = z.astype(o_ref.dtype)


def _merge(ya, yb, wa, wb, proj, ga_col, gb_col):
    m, k = ya.shape
    n = wa.shape[1]
    tm = _tile(m, 640, 8)
    tn = _tile(n, 512, _LANES)
    nj = n // tn
    return pl.pallas_call(
        _merge_kernel,
        grid=(m // tm, nj),
        in_specs=[
            pl.BlockSpec((tm, k), lambda i, j: (i, 0)),
            pl.BlockSpec((tm, k), lambda i, j: (i, 0)),
            pl.BlockSpec((k, tn), lambda i, j: (0, j)),
            pl.BlockSpec((k, tn), lambda i, j: (0, j)),
            pl.BlockSpec((tm, tn), lambda i, j: (i, ga_col * nj + j)),
            pl.BlockSpec((tm, tn), lambda i, j: (i, gb_col * nj + j)),
        ],
        out_specs=pl.BlockSpec((tm, tn), lambda i, j: (i, j)),
        out_shape=jax.ShapeDtypeStruct((m, n), _BF16),
        compiler_params=_params("parallel", "parallel"),
        name="branch_merge",
    )(ya, yb, wa, wb, proj, proj)


def _outproj_kernel(z_ref, w_ref, x_ref, o_ref):
    o_ref[...] = x_ref[...] + _dot(z_ref[...], w_ref[...])


def _outproj(z, w, x):
    m, k = z.shape
    n = w.shape[1]
    tm = _tile(m, 640, 8)
    tn = _tile(n, 1024, _LANES)
    return pl.pallas_call(
        _outproj_kernel,
        grid=(m // tm, n // tn),
        in_specs=[
            pl.BlockSpec((tm, k), lambda i, j: (i, 0)),
            pl.BlockSpec((k, tn), lambda i, j: (0, j)),
            pl.BlockSpec((tm, tn), lambda i, j: (i, j)),
        ],
        out_specs=pl.BlockSpec((tm, tn), lambda i, j: (i, j)),
        out_shape=jax.ShapeDtypeStruct((m, n), _F32),
        compiler_params=_params("parallel", "parallel"),
        name="out_proj_residual",
    )(z, w, x)


def _split_bf16(x):
    hi = x.astype(_BF16)
    lo = (x - hi.astype(_F32)).astype(_BF16)
    return hi, lo


def _norm_router_kernel(h_ref, g_ref, wr_ref, br_ref, xn_ref, lg_ref):
    h = h_ref[...]
    xn = h * lax.rsqrt(jnp.mean(h * h, axis=-1, keepdims=True) + EPS) * g_ref[...]
    xn_ref[...] = xn.astype(xn_ref.dtype)
    xh, xl = _split_bf16(xn)
    wh, wl = _split_bf16(wr_ref[...])
    lg_ref[...] = _dot(xh, wh) + _dot(xl, wh) + _dot(xh, wl) + br_ref[...]


def _norm_router(h, g, w_router, b_router):
    n, d = h.shape
    e = w_router.shape[1]
    ep = -(-e // _LANES) * _LANES
    wr = jnp.pad(w_router.astype(_F32), ((0, 0), (0, ep - e)))
    br = jnp.pad(b_router.astype(_F32), (0, ep - e)).reshape(1, ep)
    tm = _tile(n, 512, 8)
    xn, lg = pl.pallas_call(
        _norm_router_kernel,
        grid=(n // tm,),
        in_specs=[
            pl.BlockSpec((tm, d), lambda i: (i, 0)),
            pl.BlockSpec((1, d), lambda i: (0, 0)),
            pl.BlockSpec((d, ep), lambda i: (0, 0)),
            pl.BlockSpec((1, ep), lambda i: (0, 0)),
        ],
        out_specs=[pl.BlockSpec((tm, d), lambda i: (i, 0)), pl.BlockSpec((tm, ep), lambda i: (i, 0))],
        out_shape=[jax.ShapeDtypeStruct((n, d), _BF16), jax.ShapeDtypeStruct((n, ep), _F32)],
        compiler_params=_params("parallel"),
        name="ffn_norm_router",
    )(h, g.reshape(1, d), wr, br)
    return xn, lg[:, :e]


def _final_kernel(h_ref, m_ref, g_ref, o_ref):
    h = h_ref[...] + m_ref[...]
    o_ref[...] = h * lax.rsqrt(jnp.mean(h * h, axis=-1, keepdims=True) + EPS) * g_ref[...]


def _final_norm(h, moe, g):
    n, d = h.shape
    tm = _tile(n, 512, 8)
    return pl.pallas_call(
        _final_kernel,
        grid=(n // tm,),
        in_specs=[
            pl.BlockSpec((tm, d), lambda i: (i, 0)),
            pl.BlockSpec((tm, d), lambda i: (i, 0)),
            pl.BlockSpec((1, d), lambda i: (0, 0)),
        ],
        out_specs=pl.BlockSpec((tm, d), lambda i: (i, 0)),
        out_shape=jax.ShapeDtypeStruct((n, d), _F32),
        compiler_params=_params("parallel"),
        name="residual_final_norm",
    )(h, moe, g.reshape(1, d))


def _mlstm_kernel(qk_ref, v_ref, oa_ref, gcol_ref, grow_ref, cw_ref, cb_ref, gain_ref,
                  conv0_ref, c0_ref, n0_ref, m0_ref,
                  y_ref, conv_ref, c_ref, n_ref, m_ref,
                  xp_sc, qk_sc, *, nh, dk, dv, L, cw):
    c = pl.program_id(1)
    last = pl.num_programs(1) - 1
    pad = 8
    hist = cw - 1

    @pl.when(c == 0)
    def _():
        c_ref[...] = c0_ref[...]
        n_ref[...] = n0_ref[...]
        m_ref[...] = m0_ref[...]
        xp_sc[pad - hist:pad, :] = conv0_ref[0]

    @pl.when(c > 0)
    def _():
        xp_sc[pad - hist:pad, :] = xp_sc[pad + L - hist:pad + L, :]

    xp_sc[pad:pad + L, :] = qk_ref[...]

    @pl.when(c == last)
    def _():
        conv_ref[0] = xp_sc[pad + L - hist:pad + L, :]

    acc = cb_ref[...] + xp_sc[pad - hist:pad - hist + L, :] * cw_ref[0:1, :]
    for j in range(1, cw):
        acc = acc + xp_sc[pad - hist + j:pad - hist + j + L, :] * cw_ref[j:j + 1, :]
    qk_sc[...] = acc * _sigmoid(acc)

    row = lax.broadcasted_iota(jnp.int32, (L, L), 0)
    col = lax.broadcasted_iota(jnp.int32, (L, L), 1)
    causal = col <= row
    qk_off = nh * dk

    for h in range(nh):
        q = qk_sc[:, h * dk:(h + 1) * dk]
        k = qk_sc[:, qk_off + h * dk:qk_off + (h + 1) * dk] * (dk ** -0.5)
        v = v_ref[:, h * dv:(h + 1) * dv]
        f_col = gcol_ref[:, h:h + 1]
        i_col = gcol_ref[:, nh + h:nh + h + 1]
        f_row = grow_ref[0, h:h + 1, :]
        i_row = grow_ref[0, nh + h:nh + h + 1, :]
        m_prev = m_ref[0, :, h:h + 1]
        cmat = c_ref[0, h]
        nvec = n_ref[0, h:h + 1, :]

        d = jnp.where(causal, f_col - f_row + i_row, -jnp.inf)
        g = f_col + m_prev
        m_t = jnp.maximum(g, jnp.max(d, axis=1, keepdims=True))
        w = jnp.exp(d - m_t)
        a = jnp.exp(g - m_t)
        qb = q.astype(_BF16)
        vb = v.astype(_BF16)
        s = lax.dot_general(qb, k.astype(_BF16), _NT, preferred_element_type=_F32) * w
        num = a * _dot(qb, cmat.astype(_BF16)) + _dot(s.astype(_BF16), vb)
        den = a * jnp.sum(q * nvec, axis=1, keepdims=True) + jnp.sum(s, axis=1, keepdims=True)
        hh = num / jnp.maximum(jnp.abs(den), jnp.exp(-m_t))

        m_last = m_t[L - 1:L, :]
        a_last = a[L - 1:L, :]
        w_last = jnp.exp(f_col[L - 1:L, :] - f_col + i_col - m_last)
        kw = k * w_last
        c_ref[0, h] = a_last * cmat + lax.dot_general(kw.astype(_BF16), vb, _TN,
                                                      preferred_element_type=_F32)
        n_ref[0, h:h + 1, :] = a_last * nvec + jnp.sum(kw, axis=0, keepdims=True)
        m_ref[0, :, h:h + 1] = m_last

        yn = hh * lax.rsqrt(jnp.mean(hh * hh, axis=1, keepdims=True) + EPS)
        sl = slice(h * dv, (h + 1) * dv)
        y_ref[:, sl] = (yn * gain_ref[:, sl] * _sigmoid(oa_ref[:, sl])).astype(y_ref.dtype)


def _mlstm(proj, gates, row0, B, T, conv0, c0, n0, m0, conv_w, conv_b, gain_a, cols):
    nh, dk, dv = c0.shape[1], c0.shape[2], c0.shape[3]
    cw = conv_w.shape[0]
    L = min(T, CHUNK)
    nc = T // L
    qk2, va = 2 * nh * dk, nh * dv
    rb0 = row0 // L
    gt = lax.dynamic_slice_in_dim(gates, row0, B * T, axis=0)
    ig = gt[:, :nh].reshape(B, nc, L, nh)
    fcum = jnp.cumsum(jax.nn.log_sigmoid(gt[:, nh:2 * nh]).reshape(B, nc, L, nh), axis=2)
    both = jnp.concatenate([fcum, ig], axis=-1)
    gcol = both.reshape(B * T, 2 * nh)
    grow = jnp.transpose(both, (0, 1, 3, 2)).reshape(B * nc, 2 * nh, L)

    qk_cb, v_cb, oa_cb = cols
    kern = functools.partial(_mlstm_kernel, nh=nh, dk=dk, dv=dv, L=L, cw=cw)
    row_map = lambda cb: (lambda b, c: (rb0 + b * nc + c, cb))
    outs = pl.pallas_call(
        kern,
        grid=(B, nc),
        in_specs=[
            pl.BlockSpec((L, qk2), row_map(qk_cb)),
            pl.BlockSpec((L, va), row_map(v_cb)),
            pl.BlockSpec((L, va), row_map(oa_cb)),
            pl.BlockSpec((L, 2 * nh), lambda b, c: (b * nc + c, 0)),
            pl.BlockSpec((1, 2 * nh, L), lambda b, c: (b * nc + c, 0, 0)),
            pl.BlockSpec((cw, qk2), lambda b, c: (0, 0)),
            pl.BlockSpec((1, qk2), lambda b, c: (0, 0)),
            pl.BlockSpec((1, va), lambda b, c: (0, 0)),
            pl.BlockSpec((1, cw - 1, qk2), lambda b, c: (b, 0, 0)),
            pl.BlockSpec((1, nh, dk, dv), lambda b, c: (b, 0, 0, 0)),
            pl.BlockSpec((1, nh, dk), lambda b, c: (b, 0, 0)),
            pl.BlockSpec((1, 1, nh), lambda b, c: (b, 0, 0)),
        ],
        out_specs=[
            pl.BlockSpec((L, va), lambda b, c: (b * nc + c, 0)),
            pl.BlockSpec((1, cw - 1, qk2), lambda b, c: (b, 0, 0)),
            pl.BlockSpec((1, nh, dk, dv), lambda b, c: (b, 0, 0, 0)),
            pl.BlockSpec((1, nh, dk), lambda b, c: (b, 0, 0)),
            pl.BlockSpec((1, 1, nh), lambda b, c: (b, 0, 0)),
        ],
        out_shape=[
            jax.ShapeDtypeStruct((B * T, va), _BF16),
            jax.ShapeDtypeStruct((B, cw - 1, qk2), _F32),
            jax.ShapeDtypeStruct((B, nh, dk, dv), _F32),
            jax.ShapeDtypeStruct((B, nh, dk), _F32),
            jax.ShapeDtypeStruct((B, 1, nh), _F32),
        ],
        scratch_shapes=[pltpu.VMEM((L + 8, qk2), _F32), pltpu.VMEM((L, qk2), _F32)],
        compiler_params=_params("parallel", "arbitrary"),
        name="mlstm_chunks",
    )(proj, proj, proj, gcol, grow, conv_w, conv_b.reshape(1, qk2), gain_a.reshape(1, va),
      conv0, c0, n0, m0.reshape(B, 1, nh))
    y, conv_new, c_new, n_new, m_new = outs
    return y, conv_new, c_new, n_new, m_new.reshape(B, nh)


def _hgrn_sum_matrices(L):
    nlev = int(math.log2(L))
    assert 1 << nlev == L
    t = np.arange(L)[:, None]
    u = np.arange(L)[None, :]
    mats = [u <= t, u > t]
    for l in range(nlev):
        half = 1 << l
        r = (t // (2 * half)) * (2 * half) + half - 1
        upper = ((t >> l) & 1) == 1
        mats.append(np.where(upper, (u > r) & (u <= t), (u > t) & (u <= r)))
    return np.concatenate(mats, axis=0).astype(np.float32), nlev


def _hgrn_kernel(hf_ref, hi_ref, hq_ref, hg_ref, lb_ref, gain_ref, mall_ref, s0_ref,
                 y_ref, s_ref, st_sc, e_sc, *, hpg, dh, L, nlev):
    c = pl.program_id(2)
    last = pl.num_programs(2) - 1

    @pl.when(c == 0)
    def _():
        for h in range(hpg):
            st_sc[h] = s0_ref[0, h].T

    lb = lb_ref[...]
    f = lb + (1.0 - lb) * _sigmoid(hf_ref[...])
    lf = jnp.log(f)
    kk = 1.0 - f
    hq = hq_ref[...]
    qq = hq * _sigmoid(hq)
    p0 = lf.astype(_BF16)
    r1 = lf - p0.astype(_F32)
    p1 = r1.astype(_BF16)
    p2 = (r1 - p1.astype(_F32)).astype(_BF16)
    mall = mall_ref[...]
    e_sc[...] = jnp.exp(_dot(mall, p0) + _dot(mall, p1) + _dot(mall, p2))

    row = lax.broadcasted_iota(jnp.int32, (L, L), 0)
    col = lax.broadcasted_iota(jnp.int32, (L, L), 1)
    masks = []
    for l in range(nlev):
        same = (row >> (l + 1)) == (col >> (l + 1))
        masks.append(same & (((row >> l) & 1) == 1) & (((col >> l) & 1) == 0))
    diag = row == col

    for h in range(hpg):
        sl = slice(h * dh, (h + 1) * dh)
        q = qq[:, sl]
        k = kk[:, sl]
        ib = hi_ref[:, sl].astype(_BF16)
        amat = jnp.where(diag, jnp.sum(q * k, axis=1, keepdims=True), 0.0)
        for l in range(nlev):
            e = e_sc[(2 + l) * L:(3 + l) * L, sl]
            p = lax.dot_general((q * e).astype(_BF16), (k * e).astype(_BF16), _NT,
                                preferred_element_type=_F32)
            amat = amat + jnp.where(masks[l], p, 0.0)
        e_cum = e_sc[0:L, sl]
        e_rev = e_sc[L:2 * L, sl]
        st = st_sc[h]
        o = (lax.dot_general((q * e_cum).astype(_BF16), st.astype(_BF16), _NT,
                             preferred_element_type=_F32)
             + _dot(amat.astype(_BF16), ib))
        st_sc[h] = st * e_cum[L - 1:L, :] + lax.dot_general(ib, (k * e_rev).astype(_BF16), _TN,
                                                            preferred_element_type=_F32)
        yn = o * lax.rsqrt(jnp.mean(o * o, axis=1, keepdims=True) + EPS)
        hg = hg_ref[:, sl]
        y_ref[:, sl] = (yn * gain_ref[:, sl] * (hg * _sigmoid(hg))).astype(y_ref.dtype)

    @pl.when(c == last)
    def _():
        for h in range(hpg):
            s_ref[0, h] = st_sc[h].T


def _hgrn(proj, row0, B, T, s0, lb, gain_b, cols):
    nh, dh = s0.shape[1], s0.shape[2]
    wb = nh * dh
    L = min(T, CHUNK)
    nc = T // L
    rb0 = row0 // L
    hpg = min(HGRN_HEADS_PER_STEP, nh)
    ng = nh // hpg
    gw = hpg * dh
    mall_np, nlev = _hgrn_sum_matrices(L)
    mall = jnp.asarray(mall_np, dtype=_BF16)
    nrow = mall_np.shape[0]
    hf_cb, hi_cb, hq_cb, hg_cb = cols
    kern = functools.partial(_hgrn_kernel, hpg=hpg, dh=dh, L=L, nlev=nlev)
    row_map = lambda cb: (lambda b, g, c: (rb0 + b * nc + c, cb * ng + g))
    y, s_new = pl.pallas_call(
        kern,
        grid=(B, ng, nc),
        in_specs=[
            pl.BlockSpec((L, gw), row_map(hf_cb)),
            pl.BlockSpec((L, gw), row_map(hi_cb)),
            pl.BlockSpec((L, gw), row_map(hq_cb)),
            pl.BlockSpec((L, gw), row_map(hg_cb)),
            pl.BlockSpec((1, gw), lambda b, g, c: (0, g)),
            pl.BlockSpec((1, gw), lambda b, g, c: (0, g)),
            pl.BlockSpec((nrow, L), lambda b, g, c: (0, 0)),
            pl.BlockSpec((1, hpg, dh, dh), lambda b, g, c: (b, g, 0, 0)),
        ],
        out_specs=[
            pl.BlockSpec((L, gw), lambda b, g, c: (b * nc + c, g)),
            pl.BlockSpec((1, hpg, dh, dh), lambda b, g, c: (b, g, 0, 0)),
        ],
        out_shape=[
            jax.ShapeDtypeStruct((B * T, wb), _BF16),
            jax.ShapeDtypeStruct((B, nh, dh, dh), _F32),
        ],
        scratch_shapes=[pltpu.VMEM((hpg, dh, dh), _F32), pltpu.VMEM((nrow, gw), _F32)],
        compiler_params=_params("parallel", "parallel", "arbitrary"),
        name="hgrn2_chunks",
    )(proj, proj, proj, proj, lb.reshape(1, wb), gain_b.reshape(1, wb), mall, s0)
    return y, s_new


def _moe_up_kernel(blk_ref, we_ref, wj_ref, oj_ref, fl_ref,
                   x_ref, wg_ref, wu_ref, bg_ref, bu_ref, o_ref, wg_sc, wu_sc):
    w = pl.program_id(0)
    fl = fl_ref[w]

    @pl.when(fl == 3)
    def _():
        wg_sc[...] = wg_ref[...].astype(_BF16)
        wu_sc[...] = wu_ref[...].astype(_BF16)

    @pl.when(fl > 0)
    def _():
        x = x_ref[...]
        gt = jnp.minimum(_dot(x, wg_sc[...]) + bg_ref[...], SWIGLU_LIMIT)
        up = jnp.clip(_dot(x, wu_sc[...]) + bu_ref[...], -SWIGLU_LIMIT, SWIGLU_LIMIT)
        o_ref[...] = ((up + 1.0) * gt * _sigmoid(gt * SWIGLU_ALPHA)).astype(o_ref.dtype)

    @pl.when(fl == 0)
    def _():
        o_ref[...] = jnp.zeros_like(o_ref)


def _moe_down_kernel(blk_ref, we_ref, wj_ref, oj_ref, fl_ref,
                     x_ref, wd_ref, bd_ref, g_ref, o_ref, wd_sc):
    w = pl.program_id(0)
    fl = fl_ref[w]

    @pl.when(fl == 3)
    def _():
        wd_sc[...] = wd_ref[...].astype(_BF16)

    @pl.when(fl > 0)
    def _():
        o_ref[...] = (_dot(x_ref[...], wd_sc[...]) + bd_ref[...]) * g_ref[...]

    @pl.when(fl == 0)
    def _():
        o_ref[...] = jnp.zeros_like(o_ref)


def _work_tables(nblk, nb_total, nj):
    e_num = nblk.shape[0]
    blk_end = jnp.cumsum(nblk)
    blk_start = blk_end - nblk
    tot = blk_end[-1]
    w = jnp.arange(nj * nb_total, dtype=jnp.int32)
    e_w = jnp.searchsorted(nj * blk_end, w, side="right").astype(jnp.int32)
    tail = e_w >= e_num
    e_c = jnp.minimum(e_w, e_num - 1)
    nb_e = jnp.maximum(nblk[e_c], 1)
    local = w - nj * blk_start[e_c]
    j = local // nb_e
    r = local - j * nb_e
    ntail = jnp.maximum(nb_total - tot, 1)
    lt = w - nj * tot
    jt = lt // ntail
    rt = lt - jt * ntail
    e_last = e_c[jnp.maximum(nj * tot - 1, 0)]
    blk = jnp.where(tail, tot + rt, blk_start[e_c] + r)
    we = jnp.where(tail, e_last, e_c)
    wj = jnp.where(tail, nj - 1, j)
    oj = jnp.where(tail, jt, j)
    fl = jnp.where(tail, 0, jnp.where(r == 0, 3, 1))
    i32 = lambda a: a.astype(jnp.int32)
    return i32(blk), i32(we), i32(wj), i32(oj), i32(fl)


def _moe(xn, logits, w_gate_up, b_gate_up, w_down, b_down):
    nt, d = xn.shape
    e_num, _, de2 = w_gate_up.shape
    de = de2 // 2
    rows = MOE_ROWS
    na = nt * TOP_K
    top_v, top_e = lax.top_k(logits, TOP_K)
    gate = jax.nn.softmax(top_v, axis=-1)
    flat_e = top_e.reshape(-1).astype(jnp.int32)
    order = jnp.argsort(flat_e).astype(jnp.int32)
    e_sorted = flat_e[order]
    counts = jnp.sum(flat_e[:, None] == jnp.arange(e_num, dtype=jnp.int32)[None, :], axis=0).astype(jnp.int32)
    nblk = (counts + rows - 1) // rows
    blk_start = jnp.cumsum(nblk) - nblk
    sort_start = jnp.cumsum(counts) - counts
    dest = blk_start[e_sorted] * rows + jnp.arange(na, dtype=jnp.int32) - sort_start[e_sorted]
    nb_total = -(-na // rows) + e_num
    n_slots = nb_total * rows
    slot_tok = jnp.full((n_slots,), nt, jnp.int32).at[dest].set(order // TOP_K)
    slot_g = jnp.zeros((n_slots,), _F32).at[dest].set(gate.reshape(-1)[order])
    slot_of = jnp.zeros((na,), jnp.int32).at[order].set(dest).reshape(nt, TOP_K)
    x_sorted = jnp.take(xn, slot_tok, axis=0, mode="fill", fill_value=0)

    tn = _tile(de, 512, _LANES)
    nj = de // tn
    tabs = _work_tables(nblk, nb_total, nj)
    bgu = b_gate_up.reshape(e_num, 1, de2)
    act = pl.pallas_call(
        _moe_up_kernel,
        grid_spec=pltpu.PrefetchScalarGridSpec(
            num_scalar_prefetch=5,
            grid=(nj * nb_total,),
            in_specs=[
                pl.BlockSpec((rows, d), lambda w, blk, we, wj, oj, fl: (blk[w], 0)),
                pl.BlockSpec((None, d, tn), lambda w, blk, we, wj, oj, fl: (we[w], 0, wj[w])),
                pl.BlockSpec((None, d, tn), lambda w, blk, we, wj, oj, fl: (we[w], 0, nj + wj[w])),
                pl.BlockSpec((None, 1, tn), lambda w, blk, we, wj, oj, fl: (we[w], 0, wj[w])),
                pl.BlockSpec((None, 1, tn), lambda w, blk, we, wj, oj, fl: (we[w], 0, nj + wj[w])),
            ],
            out_specs=pl.BlockSpec((rows, tn), lambda w, blk, we, wj, oj, fl: (blk[w], oj[w])),
            scratch_shapes=[pltpu.VMEM((d, tn), _BF16), pltpu.VMEM((d, tn), _BF16)],
        ),
        out_shape=jax.ShapeDtypeStruct((n_slots, de), _BF16),
        compiler_params=_params("arbitrary"),
        name="moe_gate_up",
    )(*tabs, x_sorted, w_gate_up, w_gate_up, bgu, bgu)

    tn2 = _tile(d, 1024, _LANES)
    nj2 = d // tn2
    tabs2 = _work_tables(nblk, nb_total, nj2)
    y_sorted = pl.pallas_call(
        _moe_down_kernel,
        grid_spec=pltpu.PrefetchScalarGridSpec(
            num_scalar_prefetch=5,
            grid=(nj2 * nb_total,),
            in_specs=[
                pl.BlockSpec((rows, de), lambda w, blk, we, wj, oj, fl: (blk[w], 0)),
                pl.BlockSpec((None, de, tn2), lambda w, blk, we, wj, oj, fl: (we[w], 0, wj[w])),
                pl.BlockSpec((None, 1, tn2), lambda w, blk, we, wj, oj, fl: (we[w], 0, wj[w])),
                pl.BlockSpec((rows, 1), lambda w, blk, we, wj, oj, fl: (blk[w], 0)),
            ],
            out_specs=pl.BlockSpec((rows, tn2), lambda w, blk, we, wj, oj, fl: (blk[w], oj[w])),
            scratch_shapes=[pltpu.VMEM((de, tn2), _BF16)],
        ),
        out_shape=jax.ShapeDtypeStruct((n_slots, d), _F32),
        compiler_params=_params("arbitrary"),
        name="moe_down",
    )(*tabs2, act, w_down, b_down.reshape(e_num, 1, d), slot_g.reshape(n_slots, 1))

    return jnp.sum(y_sorted[slot_of], axis=1)


def kernel(x_prompt, x_sample, state_conv, state_mlstm_C, state_mlstm_n, state_mlstm_m, state_hgrn_S,
           g_mix, w_in, b_in, conv_w, conv_b, gain_a, gain_b, lb_logits, w_branch_a, w_branch_b, w_out,
           g_ffn, w_router, b_router, w_gate_up, b_gate_up, w_down, b_down, g_final):
    depth = w_in.shape[0]
    assert depth == 1, "single-layer step"
    bp, tp, d = x_prompt.shape
    bs, ts, _ = x_sample.shape
    nh_a, dk_a, dv_a = state_mlstm_C.shape[2:]
    nh_b, dh_b = state_hgrn_S.shape[2:4]
    qk2, va, wb = 2 * nh_a * dk_a, nh_a * dv_a, nh_b * dh_b
    assert qk2 == d and va == d and wb == d, "column blocks of the projection must have equal width"
    np_rows, ns_rows = bp * tp, bs * ts
    l = 0

    x = jnp.concatenate([x_prompt.reshape(np_rows, d), x_sample.reshape(ns_rows, d)], axis=0)

    sizes = (qk2, va, va, nh_a, nh_a, wb, wb, wb, wb, d, d)
    offs = np.concatenate([[0], np.cumsum(sizes)])
    big = [i for i, s in enumerate(sizes) if s == d]
    w_l, b_l = w_in[l], b_in[l]
    w_main = jnp.concatenate([w_l[:, offs[i]:offs[i + 1]] for i in big], axis=1).astype(_BF16)
    b_main = jnp.concatenate([b_l[offs[i]:offs[i + 1]] for i in big])
    ng = 2 * nh_a
    w_gate = jnp.pad(w_l[:, offs[3]:offs[5]], ((0, 0), (0, _LANES - ng))).astype(_BF16)
    b_gate = jnp.pad(b_l[offs[3]:offs[5]], (0, _LANES - ng))

    u = _rms_cast(x, g_mix[l], _BF16)
    proj = _mm_bias(u, w_main, b_main, _F32)
    gates = _mm_bias(u, w_gate, b_gate, _F32, tn_pref=_LANES)[:, :ng]

    lb = jnp.cumsum(jax.nn.softmax(lb_logits.astype(_F32), axis=0), axis=0)[l]

    def mixers(row0, B, T, conv0, c0, n0, m0, s0):
        ya, cv, cn, nn, mn = _mlstm(proj, gates, row0, B, T, conv0, c0, n0, m0,
                                    conv_w[l], conv_b[l], gain_a[l], cols=(0, 1, 2))
        yb, sn = _hgrn(proj, row0, B, T, s0, lb, gain_b[l], cols=(3, 4, 5, 6))
        return ya, yb, (cv, cn, nn, mn, sn)

    zeros = lambda *s: jnp.zeros(s, _F32)
    ya_p, yb_p, st_p = mixers(0, bp, tp, zeros(bp, conv_w.shape[1] - 1, qk2), zeros(bp, nh_a, dk_a, dv_a),
                              zeros(bp, nh_a, dk_a), zeros(bp, nh_a), zeros(bp, nh_b, dh_b, dh_b))
    ya_s, yb_s, st_s = mixers(np_rows, bs, ts, state_conv[l], state_mlstm_C[l], state_mlstm_n[l],
                              state_mlstm_m[l], state_hgrn_S[l])
    ya = jnp.concatenate([ya_p, ya_s], axis=0)
    yb = jnp.concatenate([yb_p, yb_s], axis=0)

    z = _merge(ya, yb, w_branch_a[l].astype(_BF16), w_branch_b[l].astype(_BF16), proj, 7, 8)
    h = _outproj(z, w_out[l].astype(_BF16), x)

    xn, logits = _norm_router(h, g_ffn[l], w_router[l], b_router[l])
    moe = _moe(xn, logits, w_gate_up[l], b_gate_up[l], w_down[l], b_down[l])
    out = _final_norm(h, moe, g_final)

    y_prompt = out[:np_rows].reshape(bp, tp, d)
    y_sample = out[np_rows:].reshape(bs, ts, d)
    states = [s[None] for s in st_p] + [s[None] for s in st_s]
    return (y_prompt, y_sample, *states)
```

```python
import functools
import math

import numpy as np
import jax
import jax.numpy as jnp
from jax import lax
from jax.experimental import pallas as pl
from jax.experimental.pallas import tpu as pltpu

_F32 = jnp.float32
_BF16 = jnp.bfloat16

EPS = 1e-6
TOP_K = 4
CHUNK = 64
SWIGLU_LIMIT = 7.0
SWIGLU_ALPHA = 1.702
MOE_ROWS = 512
HGRN_HEADS_PER_STEP = 8

_VMEM_LIMIT_BYTES = 56 * 1024 * 1024
_LANES = 128

_NT = (((1,), (1,)), ((), ()))
_TN = (((0,), (0,)), ((), ()))


def _params(*sem):
    return pltpu.CompilerParams(dimension_semantics=sem, vmem_limit_bytes=_VMEM_LIMIT_BYTES)


def _tile(n, pref, mult):
    best = None
    for t in range(mult, min(n, pref) + 1, mult):
        if n % t == 0:
            best = t
    return best if best is not None else n


def _sigmoid(x):
    return 1.0 / (1.0 + jnp.exp(-x))


def _dot(a, b):
    return jnp.dot(a, b, preferred_element_type=_F32)


def _rms_cast_kernel(x_ref, g_ref, o_ref):
    x = x_ref[...]
    y = x * lax.rsqrt(jnp.mean(x * x, axis=-1, keepdims=True) + EPS) * g_ref[...]
    o_ref[...] = y.astype(o_ref.dtype)


def _rms_cast(x, g, out_dtype):
    n, d = x.shape
    tm = _tile(n, 512, 8)
    return pl.pallas_call(
        _rms_cast_kernel,
        grid=(n // tm,),
        in_specs=[pl.BlockSpec((tm, d), lambda i: (i, 0)), pl.BlockSpec((1, d), lambda i: (0, 0))],
        out_specs=pl.BlockSpec((tm, d), lambda i: (i, 0)),
        out_shape=jax.ShapeDtypeStruct((n, d), out_dtype),
        compiler_params=_params("parallel"),
        name="rms_cast",
    )(x, g.reshape(1, d))


def _pack_kernel(a_ref, b_ref, o_ref, *, shift, first_shifted):
    j = pl.program_id(1)
    npiece = o_ref.shape[1] // _LANES

    @pl.when(j < first_shifted)
    def _():
        o_ref[...] = a_ref[...].astype(o_ref.dtype)

    @pl.when(j >= first_shifted)
    def _():
        lane = lax.broadcasted_iota(jnp.int32, (a_ref.shape[0], _LANES), 1)
        keep = lane < _LANES - shift
        rolled = [pltpu.roll(a_ref[:, c * _LANES:(c + 1) * _LANES], _LANES - shift, axis=1)
                  for c in range(npiece)]
        rolled.append(pltpu.roll(b_ref[...], _LANES - shift, axis=1))
        for c in range(npiece):
            o_ref[:, c * _LANES:(c + 1) * _LANES] = jnp.where(keep, rolled[c], rolled[c + 1]).astype(o_ref.dtype)


def _pack_weight(w, cut0, cut1, tn):
    k, n = w.shape
    shift = cut1 - cut0
    n_out = n - shift
    assert cut0 % tn == 0 and n_out % tn == 0 and 0 < shift < _LANES and tn % _LANES == 0
    tk = _tile(k, 512, 8)
    return pl.pallas_call(
        functools.partial(_pack_kernel, shift=shift, first_shifted=cut0 // tn),
        grid=(k // tk, n_out // tn),
        in_specs=[
            pl.BlockSpec((tk, tn), lambda i, j: (i, j)),
            pl.BlockSpec((tk, _LANES), lambda i, j: (i, (j + 1) * (tn // _LANES))),
        ],
        out_specs=pl.BlockSpec((tk, tn), lambda i, j: (i, j)),
        out_shape=jax.ShapeDtypeStruct((k, n_out), _BF16),
        compiler_params=_params("parallel", "parallel"),
        name="pack_w_in",
    )(w, w)


def _mm_kernel(x_ref, w_ref, b_ref, o_ref):
    o_ref[...] = (_dot(x_ref[...], w_ref[...]) + b_ref[...]).astype(o_ref.dtype)


def _mm_bias(x, w, b, out_dtype, tm_pref=640, tn_pref=1024):
    m, k = x.shape
    n = w.shape[1]
    tm = _tile(m, tm_pref, 8)
    tn = _tile(n, tn_pref, _LANES)
    return pl.pallas_call(
        _mm_kernel,
        grid=(m // tm, n // tn),
        in_specs=[
            pl.BlockSpec((tm, k), lambda i, j: (i, 0)),
            pl.BlockSpec((k, tn), lambda i, j: (0, j)),
            pl.BlockSpec((1, tn), lambda i, j: (0, j)),
        ],
        out_specs=pl.BlockSpec((tm, tn), lambda i, j: (i, j)),
        out_shape=jax.ShapeDtypeStruct((m, n), out_dtype),
        compiler_params=_params("parallel", "parallel"),
        name="mm_bias",
    )(x, w, b.reshape(1, n))


def _merge_kernel(ya_ref, yb_ref, wa_ref, wb_ref, ga_ref, gb_ref, o_ref):
    a = _dot(ya_ref[...], wa_ref[...])
    b = _dot(yb_ref[...], wb_ref[...])
    z = _sigmoid(ga_ref[...]) * a + _sigmoid(gb_ref[...]) * b
    o_ref[...] = z.astype(o_ref.dtype)


def _merge(ya, yb, wa, wb, proj, ga_col, gb_col):
    m, k = ya.shape
    n = wa.shape[1]
    tm = _tile(m, 640, 8)
    tn = _tile(n, 512, _LANES)
    nj = n // tn
    return pl.pallas_call(
        _merge_kernel,
        grid=(m // tm, nj),
        in_specs=[
            pl.BlockSpec((tm, k), lambda i, j: (i, 0)),
            pl.BlockSpec((tm, k), lambda i, j: (i, 0)),
            pl.BlockSpec((k, tn), lambda i, j: (0, j)),
            pl.BlockSpec((k, tn), lambda i, j: (0, j)),
            pl.BlockSpec((tm, tn), lambda i, j: (i, ga_col * nj + j)),
            pl.BlockSpec((tm, tn), lambda i, j: (i, gb_col * nj + j)),
        ],
        out_specs=pl.BlockSpec((tm, tn), lambda i, j: (i, j)),
        out_shape=jax.ShapeDtypeStruct((m, n), _BF16),
        compiler_params=_params("parallel", "parallel"),
        name="branch_merge",
    )(ya, yb, wa, wb, proj, proj)


def _outproj_kernel(z_ref, w_ref, x_ref, o_ref):
    o_ref[...] = x_ref[...] + _dot(z_ref[...], w_ref[...])


def _outproj(z, w, x):
    m, k = z.shape
    n = w.shape[1]
    tm = _tile(m, 640, 8)
    tn = _tile(n, 1024, _LANES)
    return pl.pallas_call(
        _outproj_kernel,
        grid=(m // tm, n // tn),
        in_specs=[
            pl.BlockSpec((tm, k), lambda i, j: (i, 0)),
            pl.BlockSpec((k, tn), lambda i, j: (0, j)),
            pl.BlockSpec((tm, tn), lambda i, j: (i, j)),
        ],
        out_specs=pl.BlockSpec((tm, tn), lambda i, j: (i, j)),
        out_shape=jax.ShapeDtypeStruct((m, n), _F32),
        compiler_params=_params("parallel", "parallel"),
        name="out_proj_residual",
    )(z, w, x)


def _split_bf16(x):
    hi = x.astype(_BF16)
    lo = (x - hi.astype(_F32)).astype(_BF16)
    return hi, lo


def _norm_router_kernel(h_ref, g_ref, wr_ref, br_ref, xn_ref, lg_ref):
    h = h_ref[...]
    xn = h * lax.rsqrt(jnp.mean(h * h, axis=-1, keepdims=True) + EPS) * g_ref[...]
    xn_ref[...] = xn.astype(xn_ref.dtype)
    xh, xl = _split_bf16(xn)
    wh, wl = _split_bf16(wr_ref[...])
    lg_ref[...] = _dot(xh, wh) + _dot(xl, wh) + _dot(xh, wl) + br_ref[...]


def _norm_router(h, g, w_router, b_router):
    n, d = h.shape
    e = w_router.shape[1]
    ep = -(-e // _LANES) * _LANES
    wr = jnp.pad(w_router.astype(_F32), ((0, 0), (0, ep - e)))
    br = jnp.pad(b_router.astype(_F32), (0, ep - e)).reshape(1, ep)
    tm = _tile(n, 512, 8)
    xn, lg = pl.pallas_call(
        _norm_router_kernel,
        grid=(n // tm,),
        in_specs=[
            pl.BlockSpec((tm, d), lambda i: (i, 0)),
            pl.BlockSpec((1, d), lambda i: (0, 0)),
            pl.BlockSpec((d, ep), lambda i: (0, 0)),
            pl.BlockSpec((1, ep), lambda i: (0, 0)),
        ],
        out_specs=[pl.BlockSpec((tm, d), lambda i: (i, 0)), pl.BlockSpec((tm, ep), lambda i: (i, 0))],
        out_shape=[jax.ShapeDtypeStruct((n, d), _BF16), jax.ShapeDtypeStruct((n, ep), _F32)],
        compiler_params=_params("parallel"),
        name="ffn_norm_router",
    )(h, g.reshape(1, d), wr, br)
    return xn, lg[:, :e]


def _final_kernel(h_ref, yk_ref, gate_ref, g_ref, op_ref, os_ref, *, n_prompt_tiles):
    i = pl.program_id(0)
    h = h_ref[...]
    for k in range(TOP_K):
        h = h + gate_ref[:, k:k + 1] * yk_ref[k]
    out = h * lax.rsqrt(jnp.mean(h * h, axis=-1, keepdims=True) + EPS) * g_ref[...]

    @pl.when(i < n_prompt_tiles)
    def _():
        op_ref[...] = out

    @pl.when(i >= n_prompt_tiles)
    def _():
        os_ref[...] = out


def _final_norm(h, yk, gate, g, n_prompt, n_sample):
    n, d = h.shape
    tm = _tile(math.gcd(n_prompt, n_sample), 128, 8)
    npt = n_prompt // tm
    return pl.pallas_call(
        functools.partial(_final_kernel, n_prompt_tiles=npt),
        grid=(n // tm,),
        in_specs=[
            pl.BlockSpec((tm, d), lambda i: (i, 0)),
            pl.BlockSpec((TOP_K, tm, d), lambda i: (0, i, 0)),
            pl.BlockSpec((tm, TOP_K), lambda i: (i, 0)),
            pl.BlockSpec((1, d), lambda i: (0, 0)),
        ],
        out_specs=[
            pl.BlockSpec((tm, d), lambda i: (jnp.minimum(i, npt - 1), 0)),
            pl.BlockSpec((tm, d), lambda i: (jnp.maximum(i - npt, 0), 0)),
        ],
        out_shape=[jax.ShapeDtypeStruct((n_prompt, d), _F32), jax.ShapeDtypeStruct((n_sample, d), _F32)],
        compiler_params=_params("arbitrary"),
        name="moe_combine_final_norm",
    )(h, yk, gate, g.reshape(1, d))


def _mlstm_kernel(qk_ref, v_ref, oa_ref, gcol_ref, grow_ref, cw_ref, cb_ref, gain_ref,
                  conv0_ref, c0_ref, n0_ref, m0_ref,
                  y_ref, conv_ref, c_ref, n_ref, m_ref,
                  xp_sc, qk_sc, *, nh, dk, dv, L, cw):
    c = pl.program_id(1)
    last = pl.num_programs(1) - 1
    pad = 8
    hist = cw - 1

    @pl.when(c == 0)
    def _():
        c_ref[...] = c0_ref[...]
        n_ref[...] = n0_ref[...]
        m_ref[...] = m0_ref[...]
        xp_sc[pad - hist:pad, :] = conv0_ref[0]

    @pl.when(c > 0)
    def _():
        xp_sc[pad - hist:pad, :] = xp_sc[pad + L - hist:pad + L, :]

    xp_sc[pad:pad + L, :] = qk_ref[...]

    @pl.when(c == last)
    def _():
        conv_ref[0] = xp_sc[pad + L - hist:pad + L, :]

    acc = cb_ref[...] + xp_sc[pad - hist:pad - hist + L, :] * cw_ref[0:1, :]
    for j in range(1, cw):
        acc = acc + xp_sc[pad - hist + j:pad - hist + j + L, :] * cw_ref[j:j + 1, :]
    qk_sc[...] = acc * _sigmoid(acc)

    row = lax.broadcasted_iota(jnp.int32, (L, L), 0)
    col = lax.broadcasted_iota(jnp.int32, (L, L), 1)
    causal = col <= row
    qk_off = nh * dk

    for h in range(nh):
        q = qk_sc[:, h * dk:(h + 1) * dk]
        k = qk_sc[:, qk_off + h * dk:qk_off + (h + 1) * dk] * (dk ** -0.5)
        v = v_ref[:, h * dv:(h + 1) * dv]
        f_col = gcol_ref[:, h:h + 1]
        i_col = gcol_ref[:, nh + h:nh + h + 1]
        f_row = grow_ref[0, h:h + 1, :]
        i_row = grow_ref[0, nh + h:nh + h + 1, :]
        m_prev = m_ref[0, :, h:h + 1]
        cmat = c_ref[0, h]
        nvec = n_ref[0, h:h + 1, :]

        d = jnp.where(causal, f_col - f_row + i_row, -jnp.inf)
        g = f_col + m_prev
        m_t = jnp.maximum(g, jnp.max(d, axis=1, keepdims=True))
        w = jnp.exp(d - m_t)
        a = jnp.exp(g - m_t)
        qb = q.astype(_BF16)
        vb = v.astype(_BF16)
        s = lax.dot_general(qb, k.astype(_BF16), _NT, preferred_element_type=_F32) * w
        num = a * _dot(qb, cmat.astype(_BF16)) + _dot(s.astype(_BF16), vb)
        den = a * jnp.sum(q * nvec, axis=1, keepdims=True) + jnp.sum(s, axis=1, keepdims=True)
        hh = num / jnp.maximum(jnp.abs(den), jnp.exp(-m_t))

        m_last = m_t[L - 1:L, :]
        a_last = a[L - 1:L, :]
        w_last = jnp.exp(f_col[L - 1:L, :] - f_col + i_col - m_last)
        kw = k * w_last
        c_ref[0, h] = a_last * cmat + lax.dot_general(kw.astype(_BF16), vb, _TN,
                                                      preferred_element_type=_F32)
        n_ref[0, h:h + 1, :] = a_last * nvec + jnp.sum(kw, axis=0, keepdims=True)
        m_ref[0, :, h:h + 1] = m_last

        yn = hh * lax.rsqrt(jnp.mean(hh * hh, axis=1, keepdims=True) + EPS)
        sl = slice(h * dv, (h + 1) * dv)
        y_ref[:, sl] = (yn * gain_ref[:, sl] * _sigmoid(oa_ref[:, sl])).astype(y_ref.dtype)


def _mlstm(proj, gates, row0, B, T, conv0, c0, n0, m0, conv_w, conv_b, gain_a, cols):
    nh, dk, dv = c0.shape[1], c0.shape[2], c0.shape[3]
    cw = conv_w.shape[0]
    L = min(T, CHUNK)
    nc = T // L
    qk2, va = 2 * nh * dk, nh * dv
    rb0 = row0 // L
    gt = lax.dynamic_slice_in_dim(gates, row0, B * T, axis=0)
    ig = gt[:, :nh].reshape(B, nc, L, nh)
    fcum = jnp.cumsum(jax.nn.log_sigmoid(gt[:, nh:2 * nh]).reshape(B, nc, L, nh), axis=2)
    both = jnp.concatenate([fcum, ig], axis=-1)
    gcol = both.reshape(B * T, 2 * nh)
    grow = jnp.transpose(both, (0, 1, 3, 2)).reshape(B * nc, 2 * nh, L)

    qk_cb, v_cb, oa_cb = cols
    kern = functools.partial(_mlstm_kernel, nh=nh, dk=dk, dv=dv, L=L, cw=cw)
    row_map = lambda cb: (lambda b, c: (rb0 + b * nc + c, cb))
    outs = pl.pallas_call(
        kern,
        grid=(B, nc),
        in_specs=[
            pl.BlockSpec((L, qk2), row_map(qk_cb)),
            pl.BlockSpec((L, va), row_map(v_cb)),
            pl.BlockSpec((L, va), row_map(oa_cb)),
            pl.BlockSpec((L, 2 * nh), lambda b, c: (b * nc + c, 0)),
            pl.BlockSpec((1, 2 * nh, L), lambda b, c: (b * nc + c, 0, 0)),
            pl.BlockSpec((cw, qk2), lambda b, c: (0, 0)),
            pl.BlockSpec((1, qk2), lambda b, c: (0, 0)),
            pl.BlockSpec((1, va), lambda b, c: (0, 0)),
            pl.BlockSpec((1, cw - 1, qk2), lambda b, c: (b, 0, 0)),
            pl.BlockSpec((1, nh, dk, dv), lambda b, c: (b, 0, 0, 0)),
            pl.BlockSpec((1, nh, dk), lambda b, c: (b, 0, 0)),
            pl.BlockSpec((1, 1, nh), lambda b, c: (b, 0, 0)),
        ],
        out_specs=[
            pl.BlockSpec((L, va), lambda b, c: (b * nc + c, 0)),
            pl.BlockSpec((1, cw - 1, qk2), lambda b, c: (b, 0, 0)),
            pl.BlockSpec((1, nh, dk, dv), lambda b, c: (b, 0, 0, 0)),
            pl.BlockSpec((1, nh, dk), lambda b, c: (b, 0, 0)),
            pl.BlockSpec((1, 1, nh), lambda b, c: (b, 0, 0)),
        ],
        out_shape=[
            jax.ShapeDtypeStruct((B * T, va), _BF16),
            jax.ShapeDtypeStruct((B, cw - 1, qk2), _F32),
            jax.ShapeDtypeStruct((B, nh, dk, dv), _F32),
            jax.ShapeDtypeStruct((B, nh, dk), _F32),
            jax.ShapeDtypeStruct((B, 1, nh), _F32),
        ],
        scratch_shapes=[pltpu.VMEM((L + 8, qk2), _F32), pltpu.VMEM((L, qk2), _F32)],
        compiler_params=_params("parallel", "arbitrary"),
        name="mlstm_chunks",
    )(proj, proj, proj, gcol, grow, conv_w, conv_b.reshape(1, qk2), gain_a.reshape(1, va),
      conv0, c0, n0, m0.reshape(B, 1, nh))
    y, conv_new, c_new, n_new, m_new = outs
    return y, conv_new, c_new, n_new, m_new.reshape(B, nh)


def _hgrn_sum_matrices(L):
    nlev = int(math.log2(L))
    assert 1 << nlev == L
    t = np.arange(L)[:, None]
    u = np.arange(L)[None, :]
    mats = [u <= t, u > t]
    for l in range(nlev):
        half = 1 << l
        r = (t // (2 * half)) * (2 * half) + half - 1
        upper = ((t >> l) & 1) == 1
        mats.append(np.where(upper, (u > r) & (u <= t), (u > t) & (u <= r)))
    return np.concatenate(mats, axis=0).astype(np.float32), nlev


def _hgrn_kernel(hf_ref, hi_ref, hq_ref, hg_ref, lb_ref, gain_ref, mall_ref, s0_ref,
                 y_ref, s_ref, st_sc, e_sc, *, hpg, dh, L, nlev):
    c = pl.program_id(2)
    last = pl.num_programs(2) - 1

    @pl.when(c == 0)
    def _():
        for h in range(hpg):
            st_sc[h] = s0_ref[0, h].T

    lb = lb_ref[...]
    f = lb + (1.0 - lb) * _sigmoid(hf_ref[...])
    lf = jnp.log(f)
    kk = 1.0 - f
    hq = hq_ref[...]
    qq = hq * _sigmoid(hq)
    p0 = lf.astype(_BF16)
    r1 = lf - p0.astype(_F32)
    p1 = r1.astype(_BF16)
    p2 = (r1 - p1.astype(_F32)).astype(_BF16)
    mall = mall_ref[...]
    e_sc[...] = jnp.exp(_dot(mall, p0) + _dot(mall, p1) + _dot(mall, p2))

    row = lax.broadcasted_iota(jnp.int32, (L, L), 0)
    col = lax.broadcasted_iota(jnp.int32, (L, L), 1)
    masks = []
    for l in range(nlev):
        same = (row >> (l + 1)) == (col >> (l + 1))
        masks.append(same & (((row >> l) & 1) == 1) & (((col >> l) & 1) == 0))
    diag = row == col

    for h in range(hpg):
        sl = slice(h * dh, (h + 1) * dh)
        q = qq[:, sl]
        k = kk[:, sl]
        ib = hi_ref[:, sl].astype(_BF16)
        amat = jnp.where(diag, jnp.sum(q * k, axis=1, keepdims=True), 0.0)
        for l in range(nlev):
            e = e_sc[(2 + l) * L:(3 + l) * L, sl]
            p = lax.dot_general((q * e).astype(_BF16), (k * e).astype(_BF16), _NT,
                                preferred_element_type=_F32)
            amat = amat + jnp.where(masks[l], p, 0.0)
        e_cum = e_sc[0:L, sl]
        e_rev = e_sc[L:2 * L, sl]
        st = st_sc[h]
        o = (lax.dot_general((q * e_cum).astype(_BF16), st.astype(_BF16), _NT,
                             preferred_element_type=_F32)
             + _dot(amat.astype(_BF16), ib))
        st_sc[h] = st * e_cum[L - 1:L, :] + lax.dot_general(ib, (k * e_rev).astype(_BF16), _TN,
                                                            preferred_element_type=_F32)
        yn = o * lax.rsqrt(jnp.mean(o * o, axis=1, keepdims=True) + EPS)
        hg = hg_ref[:, sl]
        y_ref[:, sl] = (yn * gain_ref[:, sl] * (hg * _sigmoid(hg))).astype(y_ref.dtype)

    @pl.when(c == last)
    def _():
        for h in range(hpg):
            s_ref[0, h] = st_sc[h].T


def _hgrn(proj, row0, B, T, s0, lb, gain_b, cols):
    nh, dh = s0.shape[1], s0.shape[2]
    wb = nh * dh
    L = min(T, CHUNK)
    nc = T // L
    rb0 = row0 // L
    hpg = min(HGRN_HEADS_PER_STEP, nh)
    ng = nh // hpg
    gw = hpg * dh
    mall_np, nlev = _hgrn_sum_matrices(L)
    mall = jnp.asarray(mall_np, dtype=_BF16)
    nrow = mall_np.shape[0]
    hf_cb, hi_cb, hq_cb, hg_cb = cols
    kern = functools.partial(_hgrn_kernel, hpg=hpg, dh=dh, L=L, nlev=nlev)
    row_map = lambda cb: (lambda b, g, c: (rb0 + b * nc + c, cb * ng + g))
    y, s_new = pl.pallas_call(
        kern,
        grid=(B, ng, nc),
        in_specs=[
            pl.BlockSpec((L, gw), row_map(hf_cb)),
            pl.BlockSpec((L, gw), row_map(hi_cb)),
            pl.BlockSpec((L, gw), row_map(hq_cb)),
            pl.BlockSpec((L, gw), row_map(hg_cb)),
            pl.BlockSpec((1, gw), lambda b, g, c: (0, g)),
            pl.BlockSpec((1, gw), lambda b, g, c: (0, g)),
            pl.BlockSpec((nrow, L), lambda b, g, c: (0, 0)),
            pl.BlockSpec((1, hpg, dh, dh), lambda b, g, c: (b, g, 0, 0)),
        ],
        out_specs=[
            pl.BlockSpec((L, gw), lambda b, g, c: (b * nc + c, g)),
            pl.BlockSpec((1, hpg, dh, dh), lambda b, g, c: (b, g, 0, 0)),
        ],
        out_shape=[
            jax.ShapeDtypeStruct((B * T, wb), _BF16),
            jax.ShapeDtypeStruct((B, nh, dh, dh), _F32),
        ],
        scratch_shapes=[pltpu.VMEM((hpg, dh, dh), _F32), pltpu.VMEM((nrow, gw), _F32)],
        compiler_params=_params("parallel", "parallel", "arbitrary"),
        name="hgrn2_chunks",
    )(proj, proj, proj, proj, lb.reshape(1, wb), gain_b.reshape(1, wb), mall, s0)
    return y, s_new


def _moe_up_kernel(blk_ref, we_ref, wj_ref, oj_ref, fl_ref,
                   x_ref, wg_ref, wu_ref, bg_ref, bu_ref, o_ref, wg_sc, wu_sc):
    w = pl.program_id(0)
    fl = fl_ref[w]

    @pl.when(fl == 3)
    def _():
        wg_sc[...] = wg_ref[...].astype(_BF16)
        wu_sc[...] = wu_ref[...].astype(_BF16)

    @pl.when(fl > 0)
    def _():
        x = x_ref[...]
        gt = jnp.minimum(_dot(x, wg_sc[...]) + bg_ref[...], SWIGLU_LIMIT)
        up = jnp.clip(_dot(x, wu_sc[...]) + bu_ref[...], -SWIGLU_LIMIT, SWIGLU_LIMIT)
        o_ref[...] = ((up + 1.0) * gt * _sigmoid(gt * SWIGLU_ALPHA)).astype(o_ref.dtype)

    @pl.when(fl == 0)
    def _():
        o_ref[...] = jnp.zeros_like(o_ref)


def _moe_down_kernel(blk_ref, we_ref, wj_ref, oj_ref, fl_ref,
                     x_ref, wd_ref, bd_ref, o_ref, wd_sc):
    w = pl.program_id(0)
    fl = fl_ref[w]

    @pl.when(fl == 3)
    def _():
        wd_sc[...] = wd_ref[...].astype(_BF16)

    @pl.when(fl > 0)
    def _():
        o_ref[...] = _dot(x_ref[...], wd_sc[...]) + bd_ref[...]

    @pl.when(fl == 0)
    def _():
        o_ref[...] = jnp.zeros_like(o_ref)


def _work_tables(nblk, nb_total, nj):
    e_num = nblk.shape[0]
    blk_end = jnp.cumsum(nblk)
    blk_start = blk_end - nblk
    tot = blk_end[-1]
    w = jnp.arange(nj * nb_total, dtype=jnp.int32)
    e_w = jnp.sum(w[:, None] >= (nj * blk_end)[None, :], axis=1).astype(jnp.int32)
    tail = e_w >= e_num
    e_c = jnp.minimum(e_w, e_num - 1)
    nb_e = jnp.maximum(nblk[e_c], 1)
    local = w - nj * blk_start[e_c]
    j = local // nb_e
    r = local - j * nb_e
    ntail = jnp.maximum(nb_total - tot, 1)
    lt = w - nj * tot
    jt = lt // ntail
    rt = lt - jt * ntail
    e_last = e_c[jnp.maximum(nj * tot - 1, 0)]
    blk = jnp.where(tail, tot + rt, blk_start[e_c] + r)
    we = jnp.where(tail, e_last, e_c)
    wj = jnp.where(tail, nj - 1, j)
    oj = jnp.where(tail, jt, j)
    fl = jnp.where(tail, 0, jnp.where(r == 0, 3, 1))
    i32 = lambda a: a.astype(jnp.int32)
    return i32(blk), i32(we), i32(wj), i32(oj), i32(fl)


def _moe(xn, logits, w_gate_up, b_gate_up, w_down, b_down):
    nt, d = xn.shape
    e_num, _, de2 = w_gate_up.shape
    de = de2 // 2
    rows = MOE_ROWS
    na = nt * TOP_K
    top_v, top_e = lax.top_k(logits, TOP_K)
    gate = jax.nn.softmax(top_v, axis=-1)
    flat_e = top_e.reshape(-1).astype(jnp.int32)
    order = jnp.argsort(flat_e).astype(jnp.int32)
    rank = jnp.argsort(order).astype(jnp.int32)
    counts = jnp.sum(flat_e[:, None] == jnp.arange(e_num, dtype=jnp.int32)[None, :], axis=0).astype(jnp.int32)
    nblk = (counts + rows - 1) // rows
    blk_end = jnp.cumsum(nblk)
    blk_start = blk_end - nblk
    sort_start = jnp.cumsum(counts) - counts
    nb_total = -(-na // rows) + e_num
    n_slots = nb_total * rows
    slot_of = blk_start[flat_e] * rows + rank - sort_start[flat_e]
    s = jnp.arange(n_slots, dtype=jnp.int32)
    e_s = jnp.minimum(jnp.sum((s // rows)[:, None] >= blk_end[None, :], axis=1), e_num - 1).astype(jnp.int32)
    pos = s - blk_start[e_s] * rows
    src = jnp.clip(sort_start[e_s] + pos, 0, na - 1)
    slot_tok = jnp.where(pos < counts[e_s], order[src] // TOP_K, 0)
    x_sorted = jnp.take(xn, slot_tok, axis=0, mode="clip")

    tn = _tile(de, 512, _LANES)
    nj = de // tn
    tabs = _work_tables(nblk, nb_total, nj)
    bgu = b_gate_up.reshape(e_num, 1, de2)
    act = pl.pallas_call(
        _moe_up_kernel,
        grid_spec=pltpu.PrefetchScalarGridSpec(
            num_scalar_prefetch=5,
            grid=(nj * nb_total,),
            in_specs=[
                pl.BlockSpec((rows, d), lambda w, blk, we, wj, oj, fl: (blk[w], 0)),
                pl.BlockSpec((None, d, tn), lambda w, blk, we, wj, oj, fl: (we[w], 0, wj[w])),
                pl.BlockSpec((None, d, tn), lambda w, blk, we, wj, oj, fl: (we[w], 0, nj + wj[w])),
                pl.BlockSpec((None, 1, tn), lambda w, blk, we, wj, oj, fl: (we[w], 0, wj[w])),
                pl.BlockSpec((None, 1, tn), lambda w, blk, we, wj, oj, fl: (we[w], 0, nj + wj[w])),
            ],
            out_specs=pl.BlockSpec((rows, tn), lambda w, blk, we, wj, oj, fl: (blk[w], oj[w])),
            scratch_shapes=[pltpu.VMEM((d, tn), _BF16), pltpu.VMEM((d, tn), _BF16)],
        ),
        out_shape=jax.ShapeDtypeStruct((n_slots, de), _BF16),
        compiler_params=_params("arbitrary"),
        name="moe_gate_up",
    )(*tabs, x_sorted, w_gate_up, w_gate_up, bgu, bgu)

    tn2 = _tile(d, 512, _LANES)
    nj2 = d // tn2
    tabs2 = _work_tables(nblk, nb_total, nj2)
    y_sorted = pl.pallas_call(
        _moe_down_kernel,
        grid_spec=pltpu.PrefetchScalarGridSpec(
            num_scalar_prefetch=5,
            grid=(nj2 * nb_total,),
            in_specs=[
                pl.BlockSpec((rows, de), lambda w, blk, we, wj, oj, fl: (blk[w], 0)),
                pl.BlockSpec((None, de, tn2), lambda w, blk, we, wj, oj, fl: (we[w], 0, wj[w])),
                pl.BlockSpec((None, 1, tn2), lambda w, blk, we, wj, oj, fl: (we[w], 0, wj[w])),
            ],
            out_specs=pl.BlockSpec((rows, tn2), lambda w, blk, we, wj, oj, fl: (blk[w], oj[w])),
            scratch_shapes=[pltpu.VMEM((de, tn2), _BF16)],
        ),
        out_shape=jax.ShapeDtypeStruct((n_slots, d), _F32),
        compiler_params=_params("arbitrary"),
        name="moe_down",
    )(*tabs2, act, w_down, b_down.reshape(e_num, 1, d))

    idx = slot_of.reshape(nt, TOP_K).T.reshape(-1)
    yk = jnp.take(y_sorted, idx, axis=0, mode="clip").reshape(TOP_K, nt, d)
    return yk, gate


def kernel(x_prompt, x_sample, state_conv, state_mlstm_C, state_mlstm_n, state_mlstm_m, state_hgrn_S,
           g_mix, w_in, b_in, conv_w, conv_b, gain_a, gain_b, lb_logits, w_branch_a, w_branch_b, w_out,
           g_ffn, w_router, b_router, w_gate_up, b_gate_up, w_down, b_down, g_final):
    depth = w_in.shape[0]
    assert depth == 1, "single-layer step"
    bp, tp, d = x_prompt.shape
    bs, ts, _ = x_sample.shape
    nh_a, dk_a, dv_a = state_mlstm_C.shape[2:]
    nh_b, dh_b = state_hgrn_S.shape[2:4]
    qk2, va, wb = 2 * nh_a * dk_a, nh_a * dv_a, nh_b * dh_b
    assert qk2 == d and va == d and wb == d, "column blocks of the projection must have equal width"
    np_rows, ns_rows = bp * tp, bs * ts
    l = 0

    x = jnp.concatenate([x_prompt.reshape(np_rows, d), x_sample.reshape(ns_rows, d)], axis=0)

    sizes = (qk2, va, va, nh_a, nh_a, wb, wb, wb, wb, d, d)
    offs = np.concatenate([[0], np.cumsum(sizes)])
    big = [i for i, s in enumerate(sizes) if s == d]
    w_l, b_l = w_in[l], b_in[l]
    w_main = _pack_weight(w_l, int(offs[3]), int(offs[5]), min(d, 2048))
    b_main = jnp.concatenate([b_l[offs[i]:offs[i + 1]] for i in big])
    ng = 2 * nh_a
    w_gate = jnp.pad(w_l[:, offs[3]:offs[5]], ((0, 0), (0, _LANES - ng))).astype(_BF16)
    b_gate = jnp.pad(b_l[offs[3]:offs[5]], (0, _LANES - ng))

    u = _rms_cast(x, g_mix[l], _BF16)
    proj = _mm_bias(u, w_main, b_main, _F32)
    gates = _mm_bias(u, w_gate, b_gate, _F32, tn_pref=_LANES)[:, :ng]

    lb = jnp.cumsum(jax.nn.softmax(lb_logits.astype(_F32), axis=0), axis=0)[l]

    def mixers(row0, B, T, conv0, c0, n0, m0, s0):
        ya, cv, cn, nn, mn = _mlstm(proj, gates, row0, B, T, conv0, c0, n0, m0,
                                    conv_w[l], conv_b[l], gain_a[l], cols=(0, 1, 2))
        yb, sn = _hgrn(proj, row0, B, T, s0, lb, gain_b[l], cols=(3, 4, 5, 6))
        return ya, yb, (cv, cn, nn, mn, sn)

    zeros = lambda *s: jnp.zeros(s, _F32)
    ya_p, yb_p, st_p = mixers(0, bp, tp, zeros(bp, conv_w.shape[1] - 1, qk2), zeros(bp, nh_a, dk_a, dv_a),
                              zeros(bp, nh_a, dk_a), zeros(bp, nh_a), zeros(bp, nh_b, dh_b, dh_b))
    ya_s, yb_s, st_s = mixers(np_rows, bs, ts, state_conv[l], state_mlstm_C[l], state_mlstm_n[l],
                              state_mlstm_m[l], state_hgrn_S[l])
    ya = jnp.concatenate([ya_p, ya_s], axis=0)
    yb = jnp.concatenate([yb_p, yb_s], axis=0)

    z = _merge(ya, yb, w_branch_a[l].astype(_BF16), w_branch_b[l].astype(_BF16), proj, 7, 8)
    h = _outproj(z, w_out[l].astype(_BF16), x)

    xn, logits = _norm_router(h, g_ffn[l], w_router[l], b_router[l])
    yk, gate = _moe(xn, logits, w_gate_up[l], b_gate_up[l], w_down[l], b_down[l])
    out_p, out_s = _final_norm(h, yk, gate, g_final, np_rows, ns_rows)

    states = [s[None] for s in st_p] + [s[None] for s in st_s]
    return (out_p.reshape(bp, tp, d), out_s.reshape(bs, ts, d), *states)
```

```python
import functools
import math

import numpy as np
import jax
import jax.numpy as jnp
from jax import lax
from jax.experimental import pallas as pl
from jax.experimental.pallas import tpu as pltpu

_F32 = jnp.float32
_BF16 = jnp.bfloat16

EPS = 1e-6
TOP_K = 4
CHUNK = 64
SWIGLU_LIMIT = 7.0
SWIGLU_ALPHA = 1.702
MOE_ROWS = 512
HGRN_HEADS_PER_STEP = 8

_VMEM_LIMIT_BYTES = 56 * 1024 * 1024
_LANES = 128

_NT = (((1,), (1,)), ((), ()))
_TN = (((0,), (0,)), ((), ()))


def _params(*sem):
    return pltpu.CompilerParams(dimension_semantics=sem, vmem_limit_bytes=_VMEM_LIMIT_BYTES)


def _tile(n, pref, mult):
    best = None
    for t in range(mult, min(n, pref) + 1, mult):
        if n % t == 0:
            best = t
    return best if best is not None else n


def _sigmoid(x):
    return 1.0 / (1.0 + jnp.exp(-x))


def _dot(a, b):
    return jnp.dot(a, b, preferred_element_type=_F32)


def _rms_cast_kernel(x_ref, g_ref, o_ref):
    x = x_ref[...]
    y = x * lax.rsqrt(jnp.mean(x * x, axis=-1, keepdims=True) + EPS) * g_ref[...]
    o_ref[...] = y.astype(o_ref.dtype)


def _rms_cast(x, g, out_dtype):
    n, d = x.shape
    tm = _tile(n, 512, 8)
    return pl.pallas_call(
        _rms_cast_kernel,
        grid=(n // tm,),
        in_specs=[pl.BlockSpec((tm, d), lambda i: (i, 0)), pl.BlockSpec((1, d), lambda i: (0, 0))],
        out_specs=pl.BlockSpec((tm, d), lambda i: (i, 0)),
        out_shape=jax.ShapeDtypeStruct((n, d), out_dtype),
        compiler_params=_params("parallel"),
        name="rms_cast",
    )(x, g.reshape(1, d))


def _pack_rows_kernel(a_ref, b_ref, o_ref, *, shift, first_shifted):
    i = pl.program_id(0)

    @pl.when(i < first_shifted)
    def _():
        o_ref[...] = a_ref[...].astype(o_ref.dtype)

    @pl.when(i >= first_shifted)
    def _():
        tb = o_ref.shape[0]
        o_ref[0:tb - shift, :] = a_ref[shift:tb, :].astype(o_ref.dtype)
        o_ref[tb - shift:tb, :] = b_ref[...].astype(o_ref.dtype)


def _pack_rows(wt, cut0, cut1, tb):
    n, k = wt.shape
    shift = cut1 - cut0
    n_out = n - shift
    assert cut0 % tb == 0 and n_out % tb == 0 and shift % 16 == 0 and tb % shift == 0
    return pl.pallas_call(
        functools.partial(_pack_rows_kernel, shift=shift, first_shifted=cut0 // tb),
        grid=(n_out // tb,),
        in_specs=[
            pl.BlockSpec((tb, k), lambda i: (i, 0)),
            pl.BlockSpec((shift, k), lambda i: ((i + 1) * (tb // shift), 0)),
        ],
        out_specs=pl.BlockSpec((tb, k), lambda i: (i, 0)),
        out_shape=jax.ShapeDtypeStruct((n_out, k), _BF16),
        compiler_params=_params("parallel"),
        name="pack_w_in",
    )(wt, wt)


def _mm_nt_kernel(x_ref, wt_ref, b_ref, o_ref):
    acc = lax.dot_general(x_ref[...], wt_ref[...], _NT, preferred_element_type=_F32)
    o_ref[...] = (acc + b_ref[...]).astype(o_ref.dtype)


def _mm_bias_nt(x, wt, b, out_dtype, tm_pref=640, tn_pref=1024):
    m, k = x.shape
    n = wt.shape[0]
    tm = _tile(m, tm_pref, 8)
    tn = _tile(n, tn_pref, _LANES)
    return pl.pallas_call(
        _mm_nt_kernel,
        grid=(m // tm, n // tn),
        in_specs=[
            pl.BlockSpec((tm, k), lambda i, j: (i, 0)),
            pl.BlockSpec((tn, k), lambda i, j: (j, 0)),
            pl.BlockSpec((1, tn), lambda i, j: (0, j)),
        ],
        out_specs=pl.BlockSpec((tm, tn), lambda i, j: (i, j)),
        out_shape=jax.ShapeDtypeStruct((m, n), out_dtype),
        compiler_params=_params("parallel", "parallel"),
        name="mm_bias",
    )(x, wt, b.reshape(1, n))


def _merge_kernel(ya_ref, yb_ref, wa_ref, wb_ref, ga_ref, gb_ref, o_ref):
    a = _dot(ya_ref[...], wa_ref[...])
    b = _dot(yb_ref[...], wb_ref[...])
    z = _sigmoid(ga_ref[...]) * a + _sigmoid(gb_ref[...]) * b
    o_ref[...] = z.astype(o_ref.dtype)


def _merge(ya, yb, wa, wb, proj, ga_col, gb_col):
    m, k = ya.shape
    n = wa.shape[1]
    tm = _tile(m, 640, 8)
    tn = _tile(n, 512, _LANES)
    nj = n // tn
    return pl.pallas_call(
        _merge_kernel,
        grid=(m // tm, nj),
        in_specs=[
            pl.BlockSpec((tm, k), lambda i, j: (i, 0)),
            pl.BlockSpec((tm, k), lambda i, j: (i, 0)),
            pl.BlockSpec((k, tn), lambda i, j: (0, j)),
            pl.BlockSpec((k, tn), lambda i, j: (0, j)),
            pl.BlockSpec((tm, tn), lambda i, j: (i, ga_col * nj + j)),
            pl.BlockSpec((tm, tn), lambda i, j: (i, gb_col * nj + j)),
        ],
        out_specs=pl.BlockSpec((tm, tn), lambda i, j: (i, j)),
        out_shape=jax.ShapeDtypeStruct((m, n), _BF16),
        compiler_params=_params("parallel", "parallel"),
        name="branch_merge",
    )(ya, yb, wa, wb, proj, proj)


def _outproj_kernel(z_ref, w_ref, x_ref, o_ref):
    o_ref[...] = x_ref[...] + _dot(z_ref[...], w_ref[...])


def _outproj(z, w, x):
    m, k = z.shape
    n = w.shape[1]
    tm = _tile(m, 640, 8)
    tn = _tile(n, 1024, _LANES)
    return pl.pallas_call(
        _outproj_kernel,
        grid=(m // tm, n // tn),
        in_specs=[
            pl.BlockSpec((tm, k), lambda i, j: (i, 0)),
            pl.BlockSpec((k, tn), lambda i, j: (0, j)),
            pl.BlockSpec((tm, tn), lambda i, j: (i, j)),
        ],
        out_specs=pl.BlockSpec((tm, tn), lambda i, j: (i, j)),
        out_shape=jax.ShapeDtypeStruct((m, n), _F32),
        compiler_params=_params("parallel", "parallel"),
        name="out_proj_residual",
    )(z, w, x)


def _split_bf16(x):
    hi = x.astype(_BF16)
    lo = (x - hi.astype(_F32)).astype(_BF16)
    return hi, lo


def _norm_router_kernel(h_ref, g_ref, wr_ref, br_ref, xn_ref, lg_ref):
    h = h_ref[...]
    xn = h * lax.rsqrt(jnp.mean(h * h, axis=-1, keepdims=True) + EPS) * g_ref[...]
    xn_ref[...] = xn.astype(xn_ref.dtype)
    xh, xl = _split_bf16(xn)
    wh, wl = _split_bf16(wr_ref[...])
    lg_ref[...] = _dot(xh, wh) + _dot(xl, wh) + _dot(xh, wl) + br_ref[...]


def _norm_router(h, g, w_router, b_router):
    n, d = h.shape
    e = w_router.shape[1]
    ep = -(-e // _LANES) * _LANES
    wr = jnp.pad(w_router.astype(_F32), ((0, 0), (0, ep - e)))
    br = jnp.pad(b_router.astype(_F32), (0, ep - e)).reshape(1, ep)
    tm = _tile(n, 512, 8)
    xn, lg = pl.pallas_call(
        _norm_router_kernel,
        grid=(n // tm,),
        in_specs=[
            pl.BlockSpec((tm, d), lambda i: (i, 0)),
            pl.BlockSpec((1, d), lambda i: (0, 0)),
            pl.BlockSpec((d, ep), lambda i: (0, 0)),
            pl.BlockSpec((1, ep), lambda i: (0, 0)),
        ],
        out_specs=[pl.BlockSpec((tm, d), lambda i: (i, 0)), pl.BlockSpec((tm, ep), lambda i: (i, 0))],
        out_shape=[jax.ShapeDtypeStruct((n, d), _BF16), jax.ShapeDtypeStruct((n, ep), _F32)],
        compiler_params=_params("parallel"),
        name="ffn_norm_router",
    )(h, g.reshape(1, d), wr, br)
    return xn, lg[:, :e]


def _final_kernel(h_ref, yk_ref, gate_ref, g_ref, op_ref, os_ref, *, n_prompt_tiles):
    i = pl.program_id(0)
    h = h_ref[...]
    for k in range(TOP_K):
        h = h + gate_ref[:, k:k + 1] * yk_ref[k]
    out = h * lax.rsqrt(jnp.mean(h * h, axis=-1, keepdims=True) + EPS) * g_ref[...]

    @pl.when(i < n_prompt_tiles)
    def _():
        op_ref[...] = out

    @pl.when(i >= n_prompt_tiles)
    def _():
        os_ref[...] = out


def _final_norm(h, yk, gate, g, n_prompt, n_sample):
    n, d = h.shape
    tm = _tile(math.gcd(n_prompt, n_sample), 128, 8)
    npt = n_prompt // tm
    return pl.pallas_call(
        functools.partial(_final_kernel, n_prompt_tiles=npt),
        grid=(n // tm,),
        in_specs=[
            pl.BlockSpec((tm, d), lambda i: (i, 0)),
            pl.BlockSpec((TOP_K, tm, d), lambda i: (0, i, 0)),
            pl.BlockSpec((tm, TOP_K), lambda i: (i, 0)),
            pl.BlockSpec((1, d), lambda i: (0, 0)),
        ],
        out_specs=[
            pl.BlockSpec((tm, d), lambda i: (jnp.minimum(i, npt - 1), 0)),
            pl.BlockSpec((tm, d), lambda i: (jnp.maximum(i - npt, 0), 0)),
        ],
        out_shape=[jax.ShapeDtypeStruct((n_prompt, d), _F32), jax.ShapeDtypeStruct((n_sample, d), _F32)],
        compiler_params=_params("arbitrary"),
        name="moe_combine_final_norm",
    )(h, yk, gate, g.reshape(1, d))


def _mlstm_kernel(qk_ref, v_ref, oa_ref, gcol_ref, grow_ref, cw_ref, cb_ref, gain_ref,
                  conv0_ref, c0_ref, n0_ref, m0_ref,
                  y_ref, conv_ref, c_ref, n_ref, m_ref,
                  xp_sc, qk_sc, *, nh, dk, dv, L, cw):
    c = pl.program_id(1)
    last = pl.num_programs(1) - 1
    pad = 8
    hist = cw - 1

    @pl.when(c == 0)
    def _():
        c_ref[...] = c0_ref[...]
        n_ref[...] = n0_ref[...]
        m_ref[...] = m0_ref[...]
        xp_sc[pad - hist:pad, :] = conv0_ref[0]

    @pl.when(c > 0)
    def _():
        xp_sc[pad - hist:pad, :] = xp_sc[pad + L - hist:pad + L, :]

    xp_sc[pad:pad + L, :] = qk_ref[...]

    @pl.when(c == last)
    def _():
        conv_ref[0] = xp_sc[pad + L - hist:pad + L, :]

    acc = cb_ref[...] + xp_sc[pad - hist:pad - hist + L, :] * cw_ref[0:1, :]
    for j in range(1, cw):
        acc = acc + xp_sc[pad - hist + j:pad - hist + j + L, :] * cw_ref[j:j + 1, :]
    qk_sc[...] = acc * _sigmoid(acc)

    row = lax.broadcasted_iota(jnp.int32, (L, L), 0)
    col = lax.broadcasted_iota(jnp.int32, (L, L), 1)
    causal = col <= row
    qk_off = nh * dk

    for h in range(nh):
        q = qk_sc[:, h * dk:(h + 1) * dk]
        k = qk_sc[:, qk_off + h * dk:qk_off + (h + 1) * dk] * (dk ** -0.5)
        v = v_ref[:, h * dv:(h + 1) * dv]
        f_col = gcol_ref[:, h:h + 1]
        i_col = gcol_ref[:, nh + h:nh + h + 1]
        f_row = grow_ref[0, h:h + 1, :]
        i_row = grow_ref[0, nh + h:nh + h + 1, :]
        m_prev = m_ref[0, :, h:h + 1]
        cmat = c_ref[0, h]
        nvec = n_ref[0, h:h + 1, :]

        d = jnp.where(causal, f_col - f_row + i_row, -jnp.inf)
        g = f_col + m_prev
        m_t = jnp.maximum(g, jnp.max(d, axis=1, keepdims=True))
        w = jnp.exp(d - m_t)
        a = jnp.exp(g - m_t)
        qb = q.astype(_BF16)
        vb = v.astype(_BF16)
        s = lax.dot_general(qb, k.astype(_BF16), _NT, preferred_element_type=_F32) * w
        num = a * _dot(qb, cmat.astype(_BF16)) + _dot(s.astype(_BF16), vb)
        den = a * jnp.sum(q * nvec, axis=1, keepdims=True) + jnp.sum(s, axis=1, keepdims=True)
        hh = num / jnp.maximum(jnp.abs(den), jnp.exp(-m_t))

        m_last = m_t[L - 1:L, :]
        a_last = a[L - 1:L, :]
        w_last = jnp.exp(f_col[L - 1:L, :] - f_col + i_col - m_last)
        kw = k * w_last
        c_ref[0, h] = a_last * cmat + lax.dot_general(kw.astype(_BF16), vb, _TN,
                                                      preferred_element_type=_F32)
        n_ref[0, h:h + 1, :] = a_last * nvec + jnp.sum(kw, axis=0, keepdims=True)
        m_ref[0, :, h:h + 1] = m_last

        yn = hh * lax.rsqrt(jnp.mean(hh * hh, axis=1, keepdims=True) + EPS)
        sl = slice(h * dv, (h + 1) * dv)
        y_ref[:, sl] = (yn * gain_ref[:, sl] * _sigmoid(oa_ref[:, sl])).astype(y_ref.dtype)


def _mlstm(proj, gates, row0, B, T, conv0, c0, n0, m0, conv_w, conv_b, gain_a, cols):
    nh, dk, dv = c0.shape[1], c0.shape[2], c0.shape[3]
    cw = conv_w.shape[0]
    L = min(T, CHUNK)
    nc = T // L
    qk2, va = 2 * nh * dk, nh * dv
    rb0 = row0 // L
    gt = lax.dynamic_slice_in_dim(gates, row0, B * T, axis=0)
    ig = gt[:, :nh].reshape(B, nc, L, nh)
    fcum = jnp.cumsum(jax.nn.log_sigmoid(gt[:, nh:2 * nh]).reshape(B, nc, L, nh), axis=2)
    both = jnp.concatenate([fcum, ig], axis=-1)
    gcol = both.reshape(B * T, 2 * nh)
    grow = jnp.transpose(both, (0, 1, 3, 2)).reshape(B * nc, 2 * nh, L)

    qk_cb, v_cb, oa_cb = cols
    kern = functools.partial(_mlstm_kernel, nh=nh, dk=dk, dv=dv, L=L, cw=cw)
    row_map = lambda cb: (lambda b, c: (rb0 + b * nc + c, cb))
    outs = pl.pallas_call(
        kern,
        grid=(B, nc),
        in_specs=[
            pl.BlockSpec((L, qk2), row_map(qk_cb)),
            pl.BlockSpec((L, va), row_map(v_cb)),
            pl.BlockSpec((L, va), row_map(oa_cb)),
            pl.BlockSpec((L, 2 * nh), lambda b, c: (b * nc + c, 0)),
            pl.BlockSpec((1, 2 * nh, L), lambda b, c: (b * nc + c, 0, 0)),
            pl.BlockSpec((cw, qk2), lambda b, c: (0, 0)),
            pl.BlockSpec((1, qk2), lambda b, c: (0, 0)),
            pl.BlockSpec((1, va), lambda b, c: (0, 0)),
            pl.BlockSpec((1, cw - 1, qk2), lambda b, c: (b, 0, 0)),
            pl.BlockSpec((1, nh, dk, dv), lambda b, c: (b, 0, 0, 0)),
            pl.BlockSpec((1, nh, dk), lambda b, c: (b, 0, 0)),
            pl.BlockSpec((1, 1, nh), lambda b, c: (b, 0, 0)),
        ],
        out_specs=[
            pl.BlockSpec((L, va), lambda b, c: (b * nc + c, 0)),
            pl.BlockSpec((1, cw - 1, qk2), lambda b, c: (b, 0, 0)),
            pl.BlockSpec((1, nh, dk, dv), lambda b, c: (b, 0, 0, 0)),
            pl.BlockSpec((1, nh, dk), lambda b, c: (b, 0, 0)),
            pl.BlockSpec((1, 1, nh), lambda b, c: (b, 0, 0)),
        ],
        out_shape=[
            jax.ShapeDtypeStruct((B * T, va), _BF16),
            jax.ShapeDtypeStruct((B, cw - 1, qk2), _F32),
            jax.ShapeDtypeStruct((B, nh, dk, dv), _F32),
            jax.ShapeDtypeStruct((B, nh, dk), _F32),
            jax.ShapeDtypeStruct((B, 1, nh), _F32),
        ],
        scratch_shapes=[pltpu.VMEM((L + 8, qk2), _F32), pltpu.VMEM((L, qk2), _F32)],
        compiler_params=_params("parallel", "arbitrary"),
        name="mlstm_chunks",
    )(proj, proj, proj, gcol, grow, conv_w, conv_b.reshape(1, qk2), gain_a.reshape(1, va),
      conv0, c0, n0, m0.reshape(B, 1, nh))
    y, conv_new, c_new, n_new, m_new = outs
    return y, conv_new, c_new, n_new, m_new.reshape(B, nh)


def _level_reference_rows(b_sc, l, L, width):
    half = 1 << l
    blk = 2 * half
    if blk >= 8:
        pieces = [jnp.broadcast_to(b_sc[i * blk + half - 1:i * blk + half, :], (blk, width))
                  for i in range(L // blk)]
    else:
        sub = lax.broadcasted_iota(jnp.int32, (8, width), 0)
        pieces = []
        for t8 in range(L // 8):
            piece = None
            for i in range(8 // blk):
                r = t8 * 8 + i * blk + half - 1
                cand = jnp.broadcast_to(b_sc[r:r + 1, :], (8, width))
                piece = cand if piece is None else jnp.where(sub >= i * blk, cand, piece)
            pieces.append(piece)
    return jnp.concatenate(pieces, axis=0)


def _hgrn_kernel(hf_ref, hi_ref, hq_ref, hg_ref, lb_ref, gain_ref, tril_ref, s0_ref,
                 y_ref, s_ref, st_sc, e_sc, b_sc, *, hpg, dh, L, nlev):
    c = pl.program_id(2)
    last = pl.num_programs(2) - 1
    width = hpg * dh

    @pl.when(c == 0)
    def _():
        for h in range(hpg):
            st_sc[h] = s0_ref[0, h].T

    lb = lb_ref[...]
    f = lb + (1.0 - lb) * _sigmoid(hf_ref[...])
    lf = jnp.log(f)
    kk = 1.0 - f
    hq = hq_ref[...]
    qq = hq * _sigmoid(hq)
    p0 = lf.astype(_BF16)
    r1 = lf - p0.astype(_F32)
    p1 = r1.astype(_BF16)
    p2 = (r1 - p1.astype(_F32)).astype(_BF16)
    tril = tril_ref[...]
    b = _dot(tril, p0) + _dot(tril, p1) + _dot(tril, p2)
    b_sc[...] = b
    e_sc[0:L, :] = jnp.exp(b)
    e_sc[L:2 * L, :] = jnp.exp(b_sc[L - 1:L, :] - b)
    rowid = lax.broadcasted_iota(jnp.int32, (L, width), 0)
    e_sc[2 * L:3 * L, :] = jnp.where((rowid & 1) == 1, f, 1.0)
    for l in range(1, nlev):
        delta = b - _level_reference_rows(b_sc, l, L, width)
        upper = ((rowid >> l) & 1) == 1
        e_sc[(2 + l) * L:(3 + l) * L, :] = jnp.exp(jnp.where(upper, delta, -delta))

    row = lax.broadcasted_iota(jnp.int32, (L, L), 0)
    col = lax.broadcasted_iota(jnp.int32, (L, L), 1)
    masks = []
    for l in range(nlev):
        same = (row >> (l + 1)) == (col >> (l + 1))
        masks.append(same & (((row >> l) & 1) == 1) & (((col >> l) & 1) == 0))
    diag = row == col

    sls = [slice(h * dh, (h + 1) * dh) for h in range(hpg)]
    ibs = [hi_ref[:, sl].astype(_BF16) for sl in sls]
    amats = []
    for sl in sls:
        q = qq[:, sl]
        k = kk[:, sl]
        amat = jnp.where(diag, jnp.sum(q * k, axis=1, keepdims=True), 0.0)
        for l in range(nlev):
            e = e_sc[(2 + l) * L:(3 + l) * L, sl]
            p = lax.dot_general((q * e).astype(_BF16), (k * e).astype(_BF16), _NT,
                                preferred_element_type=_F32)
            amat = amat + jnp.where(masks[l], p, 0.0)
        amats.append(amat.astype(_BF16))
    outs, new_states = [], []
    for h, sl in enumerate(sls):
        e_cum = e_sc[0:L, sl]
        e_rev = e_sc[L:2 * L, sl]
        st = st_sc[h]
        outs.append(lax.dot_general((qq[:, sl] * e_cum).astype(_BF16), st.astype(_BF16), _NT,
                                    preferred_element_type=_F32)
                    + _dot(amats[h], ibs[h]))
        new_states.append(st * e_cum[L - 1:L, :]
                          + lax.dot_general(ibs[h], (kk[:, sl] * e_rev).astype(_BF16), _TN,
                                            preferred_element_type=_F32))
    for h, sl in enumerate(sls):
        st_sc[h] = new_states[h]
        o = outs[h]
        yn = o * lax.rsqrt(jnp.mean(o * o, axis=1, keepdims=True) + EPS)
        hg = hg_ref[:, sl]
        y_ref[:, sl] = (yn * gain_ref[:, sl] * (hg * _sigmoid(hg))).astype(y_ref.dtype)

    @pl.when(c == last)
    def _():
        for h in range(hpg):
            s_ref[0, h] = st_sc[h].T


def _hgrn(proj, row0, B, T, s0, lb, gain_b, cols):
    nh, dh = s0.shape[1], s0.shape[2]
    wb = nh * dh
    L = min(T, CHUNK)
    nc = T // L
    rb0 = row0 // L
    hpg = min(HGRN_HEADS_PER_STEP, nh)
    ng = nh // hpg
    gw = hpg * dh
    nlev = int(math.log2(L))
    assert 1 << nlev == L and L % 8 == 0
    tril = jnp.asarray(np.tril(np.ones((L, L), np.float32)), dtype=_BF16)
    nrow = (nlev + 2) * L
    hf_cb, hi_cb, hq_cb, hg_cb = cols
    kern = functools.partial(_hgrn_kernel, hpg=hpg, dh=dh, L=L, nlev=nlev)
    row_map = lambda cb: (lambda b, g, c: (rb0 + b * nc + c, cb * ng + g))
    y, s_new = pl.pallas_call(
        kern,
        grid=(B, ng, nc),
        in_specs=[
            pl.BlockSpec((L, gw), row_map(hf_cb)),
            pl.BlockSpec((L, gw), row_map(hi_cb)),
            pl.BlockSpec((L, gw), row_map(hq_cb)),
            pl.BlockSpec((L, gw), row_map(hg_cb)),
            pl.BlockSpec((1, gw), lambda b, g, c: (0, g)),
            pl.BlockSpec((1, gw), lambda b, g, c: (0, g)),
            pl.BlockSpec((L, L), lambda b, g, c: (0, 0)),
            pl.BlockSpec((1, hpg, dh, dh), lambda b, g, c: (b, g, 0, 0)),
        ],
        out_specs=[
            pl.BlockSpec((L, gw), lambda b, g, c: (b * nc + c, g)),
            pl.BlockSpec((1, hpg, dh, dh), lambda b, g, c: (b, g, 0, 0)),
        ],
        out_shape=[
            jax.ShapeDtypeStruct((B * T, wb), _BF16),
            jax.ShapeDtypeStruct((B, nh, dh, dh), _F32),
        ],
        scratch_shapes=[pltpu.VMEM((hpg, dh, dh), _F32), pltpu.VMEM((nrow, gw), _F32),
                        pltpu.VMEM((L, gw), _F32)],
        compiler_params=_params("parallel", "parallel", "arbitrary"),
        name="hgrn2_chunks",
    )(proj, proj, proj, proj, lb.reshape(1, wb), gain_b.reshape(1, wb), tril, s0)
    return y, s_new


def _moe_up_kernel(blk_ref, we_ref, wj_ref, oj_ref, fl_ref,
                   x_ref, wg_ref, wu_ref, bg_ref, bu_ref, o_ref, wg_sc, wu_sc):
    w = pl.program_id(0)
    fl = fl_ref[w]

    @pl.when(fl == 3)
    def _():
        wg_sc[...] = wg_ref[...].astype(_BF16)
        wu_sc[...] = wu_ref[...].astype(_BF16)

    @pl.when(fl > 0)
    def _():
        x = x_ref[...]
        gt = jnp.minimum(_dot(x, wg_sc[...]) + bg_ref[...], SWIGLU_LIMIT)
        up = jnp.clip(_dot(x, wu_sc[...]) + bu_ref[...], -SWIGLU_LIMIT, SWIGLU_LIMIT)
        o_ref[...] = ((up + 1.0) * gt * _sigmoid(gt * SWIGLU_ALPHA)).astype(o_ref.dtype)

    @pl.when(fl == 0)
    def _():
        o_ref[...] = jnp.zeros_like(o_ref)


def _moe_down_kernel(blk_ref, we_ref, wj_ref, oj_ref, fl_ref,
                     x_ref, wd_ref, bd_ref, o_ref, wd_sc):
    w = pl.program_id(0)
    fl = fl_ref[w]

    @pl.when(fl == 3)
    def _():
        wd_sc[...] = wd_ref[...].astype(_BF16)

    @pl.when(fl > 0)
    def _():
        o_ref[...] = _dot(x_ref[...], wd_sc[...]) + bd_ref[...]

    @pl.when(fl == 0)
    def _():
        o_ref[...] = jnp.zeros_like(o_ref)


def _work_tables(nblk, nb_total, nj):
    e_num = nblk.shape[0]
    blk_end = jnp.cumsum(nblk)
    blk_start = blk_end - nblk
    tot = blk_end[-1]
    w = jnp.arange(nj * nb_total, dtype=jnp.int32)
    e_w = jnp.sum(w[:, None] >= (nj * blk_end)[None, :], axis=1).astype(jnp.int32)
    tail = e_w >= e_num
    e_c = jnp.minimum(e_w, e_num - 1)
    nb_e = jnp.maximum(nblk[e_c], 1)
    local = w - nj * blk_start[e_c]
    j = local // nb_e
    r = local - j * nb_e
    ntail = jnp.maximum(nb_total - tot, 1)
    lt = w - nj * tot
    jt = lt // ntail
    rt = lt - jt * ntail
    e_last = e_c[jnp.maximum(nj * tot - 1, 0)]
    blk = jnp.where(tail, tot + rt, blk_start[e_c] + r)
    we = jnp.where(tail, e_last, e_c)
    wj = jnp.where(tail, nj - 1, j)
    oj = jnp.where(tail, jt, j)
    fl = jnp.where(tail, 0, jnp.where(r == 0, 3, 1))
    i32 = lambda a: a.astype(jnp.int32)
    return i32(blk), i32(we), i32(wj), i32(oj), i32(fl)


def _moe(xn, logits, w_gate_up, b_gate_up, w_down, b_down):
    nt, d = xn.shape
    e_num, _, de2 = w_gate_up.shape
    de = de2 // 2
    rows = MOE_ROWS
    na = nt * TOP_K
    top_v, top_e = lax.top_k(logits, TOP_K)
    gate = jax.nn.softmax(top_v, axis=-1)
    flat_e = top_e.reshape(-1).astype(jnp.int32)
    order = jnp.argsort(flat_e).astype(jnp.int32)
    rank = jnp.argsort(order).astype(jnp.int32)
    counts = jnp.sum(flat_e[:, None] == jnp.arange(e_num, dtype=jnp.int32)[None, :], axis=0).astype(jnp.int32)
    nblk = (counts + rows - 1) // rows
    blk_end = jnp.cumsum(nblk)
    blk_start = blk_end - nblk
    sort_start = jnp.cumsum(counts) - counts
    nb_total = -(-na // rows) + e_num
    n_slots = nb_total * rows
    slot_of = blk_start[flat_e] * rows + rank - sort_start[flat_e]
    s = jnp.arange(n_slots, dtype=jnp.int32)
    e_s = jnp.minimum(jnp.sum((s // rows)[:, None] >= blk_end[None, :], axis=1), e_num - 1).astype(jnp.int32)
    pos = s - blk_start[e_s] * rows
    src = jnp.clip(sort_start[e_s] + pos, 0, na - 1)
    slot_tok = jnp.where(pos < counts[e_s], order[src] // TOP_K, s % nt)
    x_sorted = jnp.take(xn, slot_tok, axis=0, mode="clip")

    tn = _tile(de, 512, _LANES)
    nj = de // tn
    tabs = _work_tables(nblk, nb_total, nj)
    bgu = b_gate_up.reshape(e_num, 1, de2)
    act = pl.pallas_call(
        _moe_up_kernel,
        grid_spec=pltpu.PrefetchScalarGridSpec(
            num_scalar_prefetch=5,
            grid=(nj * nb_total,),
            in_specs=[
                pl.BlockSpec((rows, d), lambda w, blk, we, wj, oj, fl: (blk[w], 0)),
                pl.BlockSpec((None, d, tn), lambda w, blk, we, wj, oj, fl: (we[w], 0, wj[w])),
                pl.BlockSpec((None, d, tn), lambda w, blk, we, wj, oj, fl: (we[w], 0, nj + wj[w])),
                pl.BlockSpec((None, 1, tn), lambda w, blk, we, wj, oj, fl: (we[w], 0, wj[w])),
                pl.BlockSpec((None, 1, tn), lambda w, blk, we, wj, oj, fl: (we[w], 0, nj + wj[w])),
            ],
            out_specs=pl.BlockSpec((rows, tn), lambda w, blk, we, wj, oj, fl: (blk[w], oj[w])),
            scratch_shapes=[pltpu.VMEM((d, tn), _BF16), pltpu.VMEM((d, tn), _BF16)],
        ),
        out_shape=jax.ShapeDtypeStruct((n_slots, de), _BF16),
        compiler_params=_params("arbitrary"),
        name="moe_gate_up",
    )(*tabs, x_sorted, w_gate_up, w_gate_up, bgu, bgu)

    tn2 = _tile(d, 1024, _LANES)
    nj2 = d // tn2
    tabs2 = _work_tables(nblk, nb_total, nj2)
    y_sorted = pl.pallas_call(
        _moe_down_kernel,
        grid_spec=pltpu.PrefetchScalarGridSpec(
            num_scalar_prefetch=5,
            grid=(nj2 * nb_total,),
            in_specs=[
                pl.BlockSpec((rows, de), lambda w, blk, we, wj, oj, fl: (blk[w], 0)),
                pl.BlockSpec((None, de, tn2), lambda w, blk, we, wj, oj, fl: (we[w], 0, wj[w])),
                pl.BlockSpec((None, 1, tn2), lambda w, blk, we, wj, oj, fl: (we[w], 0, wj[w])),
            ],
            out_specs=pl.BlockSpec((rows, tn2), lambda w, blk, we, wj, oj, fl: (blk[w], oj[w])),
            scratch_shapes=[pltpu.VMEM((de, tn2), _BF16)],
        ),
        out_shape=jax.ShapeDtypeStruct((n_slots, d), _F32),
        compiler_params=_params("arbitrary"),
        name="moe_down",
    )(*tabs2, act, w_down, b_down.reshape(e_num, 1, d))

    idx = slot_of.reshape(nt, TOP_K).T.reshape(-1)
    yk = jnp.take(y_sorted, idx, axis=0, mode="clip").reshape(TOP_K, nt, d)
    return yk, gate


def kernel(x_prompt, x_sample, state_conv, state_mlstm_C, state_mlstm_n, state_mlstm_m, state_hgrn_S,
           g_mix, w_in, b_in, conv_w, conv_b, gain_a, gain_b, lb_logits, w_branch_a, w_branch_b, w_out,
           g_ffn, w_router, b_router, w_gate_up, b_gate_up, w_down, b_down, g_final):
    depth = w_in.shape[0]
    assert depth == 1, "single-layer step"
    bp, tp, d = x_prompt.shape
    bs, ts, _ = x_sample.shape
    nh_a, dk_a, dv_a = state_mlstm_C.shape[2:]
    nh_b, dh_b = state_hgrn_S.shape[2:4]
    qk2, va, wb = 2 * nh_a * dk_a, nh_a * dv_a, nh_b * dh_b
    assert qk2 == d and va == d and wb == d, "column blocks of the projection must have equal width"
    np_rows, ns_rows = bp * tp, bs * ts
    l = 0

    x = jnp.concatenate([x_prompt.reshape(np_rows, d), x_sample.reshape(ns_rows, d)], axis=0)

    sizes = (qk2, va, va, nh_a, nh_a, wb, wb, wb, wb, d, d)
    offs = np.concatenate([[0], np.cumsum(sizes)])
    big = [i for i, s in enumerate(sizes) if s == d]
    wt_l, b_l = jnp.swapaxes(w_in[l], 0, 1), b_in[l]
    wt_main = _pack_rows(wt_l, int(offs[3]), int(offs[5]), min(d, 512))
    b_main = jnp.concatenate([b_l[offs[i]:offs[i + 1]] for i in big])
    ng = 2 * nh_a
    wt_gate = jnp.pad(wt_l[offs[3]:offs[5], :], ((0, _LANES - ng), (0, 0))).astype(_BF16)
    b_gate = jnp.pad(b_l[offs[3]:offs[5]], (0, _LANES - ng))

    u = _rms_cast(x, g_mix[l], _BF16)
    proj = _mm_bias_nt(u, wt_main, b_main, _F32)
    gates = _mm_bias_nt(u, wt_gate, b_gate, _F32, tn_pref=_LANES)[:, :ng]

    lb = jnp.cumsum(jax.nn.softmax(lb_logits.astype(_F32), axis=0), axis=0)[l]

    def mixers(row0, B, T, conv0, c0, n0, m0, s0):
        ya, cv, cn, nn, mn = _mlstm(proj, gates, row0, B, T, conv0, c0, n0, m0,
                                    conv_w[l], conv_b[l], gain_a[l], cols=(0, 1, 2))
        yb, sn = _hgrn(proj, row0, B, T, s0, lb, gain_b[l], cols=(3, 4, 5, 6))
        return ya, yb, (cv, cn, nn, mn, sn)

    zeros = lambda *s: jnp.zeros(s, _F32)
    ya_p, yb_p, st_p = mixers(0, bp, tp, zeros(bp, conv_w.shape[1] - 1, qk2), zeros(bp, nh_a, dk_a, dv_a),
                              zeros(bp, nh_a, dk_a), zeros(bp, nh_a), zeros(bp, nh_b, dh_b, dh_b))
    ya_s, yb_s, st_s = mixers(np_rows, bs, ts, state_conv[l], state_mlstm_C[l], state_mlstm_n[l],
                              state_mlstm_m[l], state_hgrn_S[l])
    ya = jnp.concatenate([ya_p, ya_s], axis=0)
    yb = jnp.concatenate([yb_p, yb_s], axis=0)

    z = _merge(ya, yb, w_branch_a[l].astype(_BF16), w_branch_b[l].astype(_BF16), proj, 7, 8)
    h = _outproj(z, w_out[l].astype(_BF16), x)

    xn, logits = _norm_router(h, g_ffn[l], w_router[l], b_router[l])
    yk, gate = _moe(xn, logits, w_gate_up[l], b_gate_up[l], w_down[l], b_down[l])
    out_p, out_s = _final_norm(h, yk, gate, g_final, np_rows, ns_rows)

    states = [s[None] for s in st_p] + [s[None] for s in st_s]
    return (out_p.reshape(bp, tp, d), out_s.reshape(bs, ts, d), *states)
```

```python
import functools
import math

import numpy as np
import jax
import jax.numpy as jnp
from jax import lax
from jax.experimental import pallas as pl
from jax.experimental.pallas import tpu as pltpu

_F32 = jnp.float32
_BF16 = jnp.bfloat16

EPS = 1e-6
TOP_K = 4
CHUNK = 64
SWIGLU_LIMIT = 7.0
SWIGLU_ALPHA = 1.702
MOE_ROWS = 512
HGRN_HEADS_PER_STEP = 8

_VMEM_LIMIT_BYTES = 56 * 1024 * 1024
_LANES = 128

_NT = (((1,), (1,)), ((), ()))
_TN = (((0,), (0,)), ((), ()))


def _params(*sem):
    return pltpu.CompilerParams(dimension_semantics=sem, vmem_limit_bytes=_VMEM_LIMIT_BYTES)


def _tile(n, pref, mult):
    best = None
    for t in range(mult, min(n, pref) + 1, mult):
        if n % t == 0:
            best = t
    return best if best is not None else n


def _sigmoid(x):
    return 1.0 / (1.0 + jnp.exp(-x))


def _dot(a, b):
    return jnp.dot(a, b, preferred_element_type=_F32)


def _rms_cast_kernel(x_ref, g_ref, o_ref):
    x = x_ref[...]
    y = x * lax.rsqrt(jnp.mean(x * x, axis=-1, keepdims=True) + EPS) * g_ref[...]
    o_ref[...] = y.astype(o_ref.dtype)


def _rms_cast(x, g, out_dtype):
    n, d = x.shape
    tm = _tile(n, 512, 8)
    return pl.pallas_call(
        _rms_cast_kernel,
        grid=(n // tm,),
        in_specs=[pl.BlockSpec((tm, d), lambda i: (i, 0)), pl.BlockSpec((1, d), lambda i: (0, 0))],
        out_specs=pl.BlockSpec((tm, d), lambda i: (i, 0)),
        out_shape=jax.ShapeDtypeStruct((n, d), out_dtype),
        compiler_params=_params("parallel"),
        name="rms_cast",
    )(x, g.reshape(1, d))


def _pack_rows_kernel(a_ref, b_ref, o_ref, *, shift, first_shifted):
    i = pl.program_id(0)

    @pl.when(i < first_shifted)
    def _():
        o_ref[...] = a_ref[...].astype(o_ref.dtype)

    @pl.when(i >= first_shifted)
    def _():
        tb = o_ref.shape[0]
        o_ref[0:tb - shift, :] = a_ref[shift:tb, :].astype(o_ref.dtype)
        o_ref[tb - shift:tb, :] = b_ref[...].astype(o_ref.dtype)


def _pack_rows(wt, cut0, cut1, tb):
    n, k = wt.shape
    shift = cut1 - cut0
    n_out = n - shift
    assert cut0 % tb == 0 and n_out % tb == 0 and shift % 16 == 0 and tb % shift == 0
    return pl.pallas_call(
        functools.partial(_pack_rows_kernel, shift=shift, first_shifted=cut0 // tb),
        grid=(n_out // tb,),
        in_specs=[
            pl.BlockSpec((tb, k), lambda i: (i, 0)),
            pl.BlockSpec((shift, k), lambda i: ((i + 1) * (tb // shift), 0)),
        ],
        out_specs=pl.BlockSpec((tb, k), lambda i: (i, 0)),
        out_shape=jax.ShapeDtypeStruct((n_out, k), _BF16),
        compiler_params=_params("parallel"),
        name="pack_w_in",
    )(wt, wt)


def _mm_nt_kernel(x_ref, wt_ref, b_ref, o_ref):
    acc = lax.dot_general(x_ref[...], wt_ref[...], _NT, preferred_element_type=_F32)
    o_ref[...] = (acc + b_ref[...]).astype(o_ref.dtype)


def _mm_bias_nt(x, wt, b, out_dtype, tm_pref=640, tn_pref=1024):
    m, k = x.shape
    n = wt.shape[0]
    tm = _tile(m, tm_pref, 8)
    tn = _tile(n, tn_pref, _LANES)
    return pl.pallas_call(
        _mm_nt_kernel,
        grid=(m // tm, n // tn),
        in_specs=[
            pl.BlockSpec((tm, k), lambda i, j: (i, 0)),
            pl.BlockSpec((tn, k), lambda i, j: (j, 0)),
            pl.BlockSpec((1, tn), lambda i, j: (0, j)),
        ],
        out_specs=pl.BlockSpec((tm, tn), lambda i, j: (i, j)),
        out_shape=jax.ShapeDtypeStruct((m, n), out_dtype),
        compiler_params=_params("parallel", "parallel"),
        name="mm_bias",
    )(x, wt, b.reshape(1, n))


def _merge_kernel(ya_ref, yb_ref, wa_ref, wb_ref, ga_ref, gb_ref, o_ref):
    a = _dot(ya_ref[...], wa_ref[...])
    b = _dot(yb_ref[...], wb_ref[...])
    z = _sigmoid(ga_ref[...]) * a + _sigmoid(gb_ref[...]) * b
    o_ref[...] = z.astype(o_ref.dtype)


def _merge(ya, yb, wa, wb, proj, ga_col, gb_col):
    m, k = ya.shape
    n = wa.shape[1]
    tm = _tile(m, 640, 8)
    tn = _tile(n, 512, _LANES)
    nj = n // tn
    return pl.pallas_call(
        _merge_kernel,
        grid=(m // tm, nj),
        in_specs=[
            pl.BlockSpec((tm, k), lambda i, j: (i, 0)),
            pl.BlockSpec((tm, k), lambda i, j: (i, 0)),
            pl.BlockSpec((k, tn), lambda i, j: (0, j)),
            pl.BlockSpec((k, tn), lambda i, j: (0, j)),
            pl.BlockSpec((tm, tn), lambda i, j: (i, ga_col * nj + j)),
            pl.BlockSpec((tm, tn), lambda i, j: (i, gb_col * nj + j)),
        ],
        out_specs=pl.BlockSpec((tm, tn), lambda i, j: (i, j)),
        out_shape=jax.ShapeDtypeStruct((m, n), _BF16),
        compiler_params=_params("parallel", "parallel"),
        name="branch_merge",
    )(ya, yb, wa, wb, proj, proj)


def _outproj_kernel(z_ref, w_ref, x_ref, o_ref):
    o_ref[...] = x_ref[...] + _dot(z_ref[...], w_ref[...])


def _outproj(z, w, x):
    m, k = z.shape
    n = w.shape[1]
    tm = _tile(m, 640, 8)
    tn = _tile(n, 1024, _LANES)
    return pl.pallas_call(
        _outproj_kernel,
        grid=(m // tm, n // tn),
        in_specs=[
            pl.BlockSpec((tm, k), lambda i, j: (i, 0)),
            pl.BlockSpec((k, tn), lambda i, j: (0, j)),
            pl.BlockSpec((tm, tn), lambda i, j: (i, j)),
        ],
        out_specs=pl.BlockSpec((tm, tn), lambda i, j: (i, j)),
        out_shape=jax.ShapeDtypeStruct((m, n), _F32),
        compiler_params=_params("parallel", "parallel"),
        name="out_proj_residual",
    )(z, w, x)


def _split_bf16(x):
    hi = x.astype(_BF16)
    lo = (x - hi.astype(_F32)).astype(_BF16)
    return hi, lo


def _norm_router_kernel(h_ref, g_ref, wr_ref, br_ref, xn_ref, lg_ref):
    h = h_ref[...]
    xn = h * lax.rsqrt(jnp.mean(h * h, axis=-1, keepdims=True) + EPS) * g_ref[...]
    xn_ref[...] = xn.astype(xn_ref.dtype)
    xh, xl = _split_bf16(xn)
    wh, wl = _split_bf16(wr_ref[...])
    lg_ref[...] = _dot(xh, wh) + _dot(xl, wh) + _dot(xh, wl) + br_ref[...]


def _norm_router(h, g, w_router, b_router):
    n, d = h.shape
    e = w_router.shape[1]
    ep = -(-e // _LANES) * _LANES
    wr = jnp.pad(w_router.astype(_F32), ((0, 0), (0, ep - e)))
    br = jnp.pad(b_router.astype(_F32), (0, ep - e)).reshape(1, ep)
    tm = _tile(n, 512, 8)
    xn, lg = pl.pallas_call(
        _norm_router_kernel,
        grid=(n // tm,),
        in_specs=[
            pl.BlockSpec((tm, d), lambda i: (i, 0)),
            pl.BlockSpec((1, d), lambda i: (0, 0)),
            pl.BlockSpec((d, ep), lambda i: (0, 0)),
            pl.BlockSpec((1, ep), lambda i: (0, 0)),
        ],
        out_specs=[pl.BlockSpec((tm, d), lambda i: (i, 0)), pl.BlockSpec((tm, ep), lambda i: (i, 0))],
        out_shape=[jax.ShapeDtypeStruct((n, d), _BF16), jax.ShapeDtypeStruct((n, ep), _F32)],
        compiler_params=_params("parallel"),
        name="ffn_norm_router",
    )(h, g.reshape(1, d), wr, br)
    return xn, lg[:, :e]


def _final_kernel(h_ref, yk_ref, gate_ref, g_ref, op_ref, os_ref, *, n_prompt_tiles):
    i = pl.program_id(0)
    h = h_ref[...]
    for k in range(TOP_K):
        h = h + gate_ref[:, k:k + 1] * yk_ref[k]
    out = h * lax.rsqrt(jnp.mean(h * h, axis=-1, keepdims=True) + EPS) * g_ref[...]

    @pl.when(i < n_prompt_tiles)
    def _():
        op_ref[...] = out

    @pl.when(i >= n_prompt_tiles)
    def _():
        os_ref[...] = out


def _final_norm(h, yk, gate, g, n_prompt, n_sample):
    n, d = h.shape
    tm = _tile(math.gcd(n_prompt, n_sample), 128, 8)
    npt = n_prompt // tm
    return pl.pallas_call(
        functools.partial(_final_kernel, n_prompt_tiles=npt),
        grid=(n // tm,),
        in_specs=[
            pl.BlockSpec((tm, d), lambda i: (i, 0)),
            pl.BlockSpec((TOP_K, tm, d), lambda i: (0, i, 0)),
            pl.BlockSpec((tm, TOP_K), lambda i: (i, 0)),
            pl.BlockSpec((1, d), lambda i: (0, 0)),
        ],
        out_specs=[
            pl.BlockSpec((tm, d), lambda i: (jnp.minimum(i, npt - 1), 0)),
            pl.BlockSpec((tm, d), lambda i: (jnp.maximum(i - npt, 0), 0)),
        ],
        out_shape=[jax.ShapeDtypeStruct((n_prompt, d), _F32), jax.ShapeDtypeStruct((n_sample, d), _F32)],
        compiler_params=_params("arbitrary"),
        name="moe_combine_final_norm",
    )(h, yk, gate, g.reshape(1, d))


def _mlstm_kernel(qk_ref, v_ref, oa_ref, gcol_ref, grow_ref, cw_ref, cb_ref, gain_ref,
                  conv0_ref, c0_ref, n0_ref, m0_ref,
                  y_ref, conv_ref, c_ref, n_ref, m_ref,
                  xp_sc, qk_sc, *, nh, dk, dv, L, cw):
    c = pl.program_id(1)
    last = pl.num_programs(1) - 1
    pad = 8
    hist = cw - 1

    @pl.when(c == 0)
    def _():
        c_ref[...] = c0_ref[...]
        n_ref[...] = n0_ref[...]
        m_ref[...] = m0_ref[...]
        xp_sc[pad - hist:pad, :] = conv0_ref[0]

    @pl.when(c > 0)
    def _():
        xp_sc[pad - hist:pad, :] = xp_sc[pad + L - hist:pad + L, :]

    xp_sc[pad:pad + L, :] = qk_ref[...]

    @pl.when(c == last)
    def _():
        conv_ref[0] = xp_sc[pad + L - hist:pad + L, :]

    acc = cb_ref[...] + xp_sc[pad - hist:pad - hist + L, :] * cw_ref[0:1, :]
    for j in range(1, cw):
        acc = acc + xp_sc[pad - hist + j:pad - hist + j + L, :] * cw_ref[j:j + 1, :]
    qk_sc[...] = acc * _sigmoid(acc)

    row = lax.broadcasted_iota(jnp.int32, (L, L), 0)
    col = lax.broadcasted_iota(jnp.int32, (L, L), 1)
    causal = col <= row
    qk_off = nh * dk

    pending = []
    for h in range(nh):
        q = qk_sc[:, h * dk:(h + 1) * dk]
        k = qk_sc[:, qk_off + h * dk:qk_off + (h + 1) * dk] * (dk ** -0.5)
        v = v_ref[:, h * dv:(h + 1) * dv]
        f_col = gcol_ref[:, h:h + 1]
        i_col = gcol_ref[:, nh + h:nh + h + 1]
        f_row = grow_ref[0, h:h + 1, :]
        i_row = grow_ref[0, nh + h:nh + h + 1, :]
        m_prev = m_ref[0, :, h:h + 1]
        cmat = c_ref[0, h]
        nvec = n_ref[0, h:h + 1, :]

        d = jnp.where(causal, f_col - f_row + i_row, -jnp.inf)
        g = f_col + m_prev
        m_t = jnp.maximum(g, jnp.max(d, axis=1, keepdims=True))
        w = jnp.exp(d - m_t)
        a = jnp.exp(g - m_t)
        qb = q.astype(_BF16)
        vb = v.astype(_BF16)
        s = lax.dot_general(qb, k.astype(_BF16), _NT, preferred_element_type=_F32) * w
        num = a * _dot(qb, cmat.astype(_BF16)) + _dot(s.astype(_BF16), vb)
        den = a * jnp.sum(q * nvec, axis=1, keepdims=True) + jnp.sum(s, axis=1, keepdims=True)
        hh = num / jnp.maximum(jnp.abs(den), jnp.exp(-m_t))

        m_last = m_t[L - 1:L, :]
        a_last = a[L - 1:L, :]
        w_last = jnp.exp(f_col[L - 1:L, :] - f_col + i_col - m_last)
        kw = k * w_last
        yn = hh * lax.rsqrt(jnp.mean(hh * hh, axis=1, keepdims=True) + EPS)
        sl = slice(h * dv, (h + 1) * dv)
        pending.append((
            (yn * gain_ref[:, sl] * _sigmoid(oa_ref[:, sl])).astype(y_ref.dtype),
            lax.dot_general(kw.astype(_BF16), vb, _TN, preferred_element_type=_F32),
            a_last, a_last * nvec + jnp.sum(kw, axis=0, keepdims=True), m_last))

    for h, (y_h, kv_h, a_last, n_new, m_last) in enumerate(pending):
        y_ref[:, h * dv:(h + 1) * dv] = y_h
        c_ref[0, h] = a_last * c_ref[0, h] + kv_h
        n_ref[0, h:h + 1, :] = n_new
        m_ref[0, :, h:h + 1] = m_last


def _mlstm(proj, gates, row0, B, T, conv0, c0, n0, m0, conv_w, conv_b, gain_a, cols):
    nh, dk, dv = c0.shape[1], c0.shape[2], c0.shape[3]
    cw = conv_w.shape[0]
    L = min(T, CHUNK)
    nc = T // L
    qk2, va = 2 * nh * dk, nh * dv
    rb0 = row0 // L
    gt = lax.dynamic_slice_in_dim(gates, row0, B * T, axis=0)
    ig = gt[:, :nh].reshape(B, nc, L, nh)
    fcum = jnp.cumsum(jax.nn.log_sigmoid(gt[:, nh:2 * nh]).reshape(B, nc, L, nh), axis=2)
    both = jnp.concatenate([fcum, ig], axis=-1)
    gcol = both.reshape(B * T, 2 * nh)
    grow = jnp.transpose(both, (0, 1, 3, 2)).reshape(B * nc, 2 * nh, L)

    qk_cb, v_cb, oa_cb = cols
    kern = functools.partial(_mlstm_kernel, nh=nh, dk=dk, dv=dv, L=L, cw=cw)
    row_map = lambda cb: (lambda b, c: (rb0 + b * nc + c, cb))
    outs = pl.pallas_call(
        kern,
        grid=(B, nc),
        in_specs=[
            pl.BlockSpec((L, qk2), row_map(qk_cb)),
            pl.BlockSpec((L, va), row_map(v_cb)),
            pl.BlockSpec((L, va), row_map(oa_cb)),
            pl.BlockSpec((L, 2 * nh), lambda b, c: (b * nc + c, 0)),
            pl.BlockSpec((1, 2 * nh, L), lambda b, c: (b * nc + c, 0, 0)),
            pl.BlockSpec((cw, qk2), lambda b, c: (0, 0)),
            pl.BlockSpec((1, qk2), lambda b, c: (0, 0)),
            pl.BlockSpec((1, va), lambda b, c: (0, 0)),
            pl.BlockSpec((1, cw - 1, qk2), lambda b, c: (b, 0, 0)),
            pl.BlockSpec((1, nh, dk, dv), lambda b, c: (b, 0, 0, 0)),
            pl.BlockSpec((1, nh, dk), lambda b, c: (b, 0, 0)),
            pl.BlockSpec((1, 1, nh), lambda b, c: (b, 0, 0)),
        ],
        out_specs=[
            pl.BlockSpec((L, va), lambda b, c: (b * nc + c, 0)),
            pl.BlockSpec((1, cw - 1, qk2), lambda b, c: (b, 0, 0)),
            pl.BlockSpec((1, nh, dk, dv), lambda b, c: (b, 0, 0, 0)),
            pl.BlockSpec((1, nh, dk), lambda b, c: (b, 0, 0)),
            pl.BlockSpec((1, 1, nh), lambda b, c: (b, 0, 0)),
        ],
        out_shape=[
            jax.ShapeDtypeStruct((B * T, va), _BF16),
            jax.ShapeDtypeStruct((B, cw - 1, qk2), _F32),
            jax.ShapeDtypeStruct((B, nh, dk, dv), _F32),
            jax.ShapeDtypeStruct((B, nh, dk), _F32),
            jax.ShapeDtypeStruct((B, 1, nh), _F32),
        ],
        scratch_shapes=[pltpu.VMEM((L + 8, qk2), _F32), pltpu.VMEM((L, qk2), _F32)],
        compiler_params=_params("parallel", "arbitrary"),
        name="mlstm_chunks",
    )(proj, proj, proj, gcol, grow, conv_w, conv_b.reshape(1, qk2), gain_a.reshape(1, va),
      conv0, c0, n0, m0.reshape(B, 1, nh))
    y, conv_new, c_new, n_new, m_new = outs
    return y, conv_new, c_new, n_new, m_new.reshape(B, nh)


def _level_reference_rows(b_sc, l, L, width):
    half = 1 << l
    blk = 2 * half
    if blk >= 8:
        pieces = [jnp.broadcast_to(b_sc[i * blk + half - 1:i * blk + half, :], (blk, width))
                  for i in range(L // blk)]
    else:
        sub = lax.broadcasted_iota(jnp.int32, (8, width), 0)
        pieces = []
        for t8 in range(L // 8):
            piece = None
            for i in range(8 // blk):
                r = t8 * 8 + i * blk + half - 1
                cand = jnp.broadcast_to(b_sc[r:r + 1, :], (8, width))
                piece = cand if piece is None else jnp.where(sub >= i * blk, cand, piece)
            pieces.append(piece)
    return jnp.concatenate(pieces, axis=0)


def _hgrn_kernel(hf_ref, hi_ref, hq_ref, hg_ref, lb_ref, gain_ref, tril_ref, s0_ref,
                 y_ref, s_ref, st_sc, e_sc, b_sc, *, hpg, dh, L, nlev):
    c = pl.program_id(2)
    last = pl.num_programs(2) - 1
    width = hpg * dh

    @pl.when(c == 0)
    def _():
        for h in range(hpg):
            st_sc[h] = s0_ref[0, h].T

    lb = lb_ref[...]
    f = lb + (1.0 - lb) * _sigmoid(hf_ref[...])
    lf = jnp.log(f)
    kk = 1.0 - f
    hq = hq_ref[...]
    qq = hq * _sigmoid(hq)
    p0 = lf.astype(_BF16)
    r1 = lf - p0.astype(_F32)
    p1 = r1.astype(_BF16)
    p2 = (r1 - p1.astype(_F32)).astype(_BF16)
    tril = tril_ref[...]
    b = _dot(tril, p0) + _dot(tril, p1) + _dot(tril, p2)
    b_sc[...] = b
    e_sc[0:L, :] = jnp.exp(b)
    e_sc[L:2 * L, :] = jnp.exp(b_sc[L - 1:L, :] - b)
    rowid = lax.broadcasted_iota(jnp.int32, (L, width), 0)
    e_sc[2 * L:3 * L, :] = jnp.where((rowid & 1) == 1, f, 1.0)
    for l in range(1, nlev):
        delta = b - _level_reference_rows(b_sc, l, L, width)
        upper = ((rowid >> l) & 1) == 1
        e_sc[(2 + l) * L:(3 + l) * L, :] = jnp.exp(jnp.where(upper, delta, -delta))

    row = lax.broadcasted_iota(jnp.int32, (L, L), 0)
    col = lax.broadcasted_iota(jnp.int32, (L, L), 1)
    masks = []
    for l in range(nlev):
        same = (row >> (l + 1)) == (col >> (l + 1))
        masks.append(same & (((row >> l) & 1) == 1) & (((col >> l) & 1) == 0))
    diag = row == col

    sls = [slice(h * dh, (h + 1) * dh) for h in range(hpg)]
    ibs = [hi_ref[:, sl].astype(_BF16) for sl in sls]
    amats = []
    for sl in sls:
        q = qq[:, sl]
        k = kk[:, sl]
        amat = jnp.where(diag, jnp.sum(q * k, axis=1, keepdims=True), 0.0)
        for l in range(nlev):
            e = e_sc[(2 + l) * L:(3 + l) * L, sl]
            p = lax.dot_general((q * e).astype(_BF16), (k * e).astype(_BF16), _NT,
                                preferred_element_type=_F32)
            amat = jnp.where(masks[l], p, amat)
        amats.append(amat.astype(_BF16))
    outs, new_states = [], []
    for h, sl in enumerate(sls):
        e_cum = e_sc[0:L, sl]
        e_rev = e_sc[L:2 * L, sl]
        st = st_sc[h]
        outs.append(lax.dot_general((qq[:, sl] * e_cum).astype(_BF16), st.astype(_BF16), _NT,
                                    preferred_element_type=_F32)
                    + _dot(amats[h], ibs[h]))
        new_states.append(st * e_cum[L - 1:L, :]
                          + lax.dot_general(ibs[h], (kk[:, sl] * e_rev).astype(_BF16), _TN,
                                            preferred_element_type=_F32))
    for h, sl in enumerate(sls):
        st_sc[h] = new_states[h]
        o = outs[h]
        yn = o * lax.rsqrt(jnp.mean(o * o, axis=1, keepdims=True) + EPS)
        hg = hg_ref[:, sl]
        y_ref[:, sl] = (yn * gain_ref[:, sl] * (hg * _sigmoid(hg))).astype(y_ref.dtype)

    @pl.when(c == last)
    def _():
        for h in range(hpg):
            s_ref[0, h] = st_sc[h].T


def _hgrn(proj, row0, B, T, s0, lb, gain_b, cols):
    nh, dh = s0.shape[1], s0.shape[2]
    wb = nh * dh
    L = min(T, CHUNK)
    nc = T // L
    rb0 = row0 // L
    hpg = min(HGRN_HEADS_PER_STEP, nh)
    ng = nh // hpg
    gw = hpg * dh
    nlev = int(math.log2(L))
    assert 1 << nlev == L and L % 8 == 0
    tril = jnp.asarray(np.tril(np.ones((L, L), np.float32)), dtype=_BF16)
    nrow = (nlev + 2) * L
    hf_cb, hi_cb, hq_cb, hg_cb = cols
    kern = functools.partial(_hgrn_kernel, hpg=hpg, dh=dh, L=L, nlev=nlev)
    row_map = lambda cb: (lambda b, g, c: (rb0 + b * nc + c, cb * ng + g))
    y, s_new = pl.pallas_call(
        kern,
        grid=(B, ng, nc),
        in_specs=[
            pl.BlockSpec((L, gw), row_map(hf_cb)),
            pl.BlockSpec((L, gw), row_map(hi_cb)),
            pl.BlockSpec((L, gw), row_map(hq_cb)),
            pl.BlockSpec((L, gw), row_map(hg_cb)),
            pl.BlockSpec((1, gw), lambda b, g, c: (0, g)),
            pl.BlockSpec((1, gw), lambda b, g, c: (0, g)),
            pl.BlockSpec((L, L), lambda b, g, c: (0, 0)),
            pl.BlockSpec((1, hpg, dh, dh), lambda b, g, c: (b, g, 0, 0)),
        ],
        out_specs=[
            pl.BlockSpec((L, gw), lambda b, g, c: (b * nc + c, g)),
            pl.BlockSpec((1, hpg, dh, dh), lambda b, g, c: (b, g, 0, 0)),
        ],
        out_shape=[
            jax.ShapeDtypeStruct((B * T, wb), _BF16),
            jax.ShapeDtypeStruct((B, nh, dh, dh), _F32),
        ],
        scratch_shapes=[pltpu.VMEM((hpg, dh, dh), _F32), pltpu.VMEM((nrow, gw), _F32),
                        pltpu.VMEM((L, gw), _F32)],
        compiler_params=_params("parallel", "parallel", "arbitrary"),
        name="hgrn2_chunks",
    )(proj, proj, proj, proj, lb.reshape(1, wb), gain_b.reshape(1, wb), tril, s0)
    return y, s_new


_FLAG_COMPUTE, _FLAG_CAST, _FLAG_HALF = 1, 2, 4


def _swiglu_block(x, wg, wu, bg, bu, out_dtype):
    gt = jnp.minimum(_dot(x, wg) + bg, SWIGLU_LIMIT)
    up = jnp.clip(_dot(x, wu) + bu, -SWIGLU_LIMIT, SWIGLU_LIMIT)
    return ((up + 1.0) * gt * _sigmoid(gt * SWIGLU_ALPHA)).astype(out_dtype)


def _moe_up_kernel(blk_ref, we_ref, wj_ref, oj_ref, fl_ref,
                   x_ref, wg_ref, wu_ref, bg_ref, bu_ref, o_ref, wg_sc, wu_sc):
    w = pl.program_id(0)
    fl = fl_ref[w]
    half = o_ref.shape[0] // 2

    @pl.when((fl & _FLAG_CAST) != 0)
    def _():
        wg_sc[...] = wg_ref[...].astype(_BF16)
        wu_sc[...] = wu_ref[...].astype(_BF16)

    @pl.when((fl & (_FLAG_COMPUTE | _FLAG_HALF)) == _FLAG_COMPUTE)
    def _():
        o_ref[...] = _swiglu_block(x_ref[...], wg_sc[...], wu_sc[...], bg_ref[...], bu_ref[...], o_ref.dtype)

    @pl.when((fl & (_FLAG_COMPUTE | _FLAG_HALF)) == (_FLAG_COMPUTE | _FLAG_HALF))
    def _():
        o_ref[0:half, :] = _swiglu_block(x_ref[0:half, :], wg_sc[...], wu_sc[...], bg_ref[...], bu_ref[...],
                                         o_ref.dtype)
        o_ref[half:, :] = jnp.zeros((o_ref.shape[0] - half, o_ref.shape[1]), o_ref.dtype)

    @pl.when((fl & _FLAG_COMPUTE) == 0)
    def _():
        o_ref[...] = jnp.zeros_like(o_ref)


def _moe_down_kernel(blk_ref, we_ref, wj_ref, oj_ref, fl_ref,
                     x_ref, wd_ref, bd_ref, o_ref, wd_sc):
    w = pl.program_id(0)
    fl = fl_ref[w]
    half = o_ref.shape[0] // 2

    @pl.when((fl & _FLAG_CAST) != 0)
    def _():
        wd_sc[...] = wd_ref[...].astype(_BF16)

    @pl.when((fl & (_FLAG_COMPUTE | _FLAG_HALF)) == _FLAG_COMPUTE)
    def _():
        o_ref[...] = _dot(x_ref[...], wd_sc[...]) + bd_ref[...]

    @pl.when((fl & (_FLAG_COMPUTE | _FLAG_HALF)) == (_FLAG_COMPUTE | _FLAG_HALF))
    def _():
        o_ref[0:half, :] = _dot(x_ref[0:half, :], wd_sc[...]) + bd_ref[...]
        o_ref[half:, :] = jnp.zeros((o_ref.shape[0] - half, o_ref.shape[1]), o_ref.dtype)

    @pl.when((fl & _FLAG_COMPUTE) == 0)
    def _():
        o_ref[...] = jnp.zeros_like(o_ref)


def _work_tables(counts, rows, nb_total, nj):
    nblk = (counts + rows - 1) // rows
    e_num = nblk.shape[0]
    blk_end = jnp.cumsum(nblk)
    blk_start = blk_end - nblk
    tot = blk_end[-1]
    w = jnp.arange(nj * nb_total, dtype=jnp.int32)
    e_w = jnp.sum(w[:, None] >= (nj * blk_end)[None, :], axis=1).astype(jnp.int32)
    tail = e_w >= e_num
    e_c = jnp.minimum(e_w, e_num - 1)
    nb_e = jnp.maximum(nblk[e_c], 1)
    local = w - nj * blk_start[e_c]
    j = local // nb_e
    r = local - j * nb_e
    ntail = jnp.maximum(nb_total - tot, 1)
    lt = w - nj * tot
    jt = lt // ntail
    rt = lt - jt * ntail
    e_last = e_c[jnp.maximum(nj * tot - 1, 0)]
    blk = jnp.where(tail, tot + rt, blk_start[e_c] + r)
    we = jnp.where(tail, e_last, e_c)
    wj = jnp.where(tail, nj - 1, j)
    oj = jnp.where(tail, jt, j)
    half = (counts[e_c] - r * rows) <= rows // 2
    fl = jnp.where(tail, 0, _FLAG_COMPUTE + jnp.where(r == 0, _FLAG_CAST, 0) + jnp.where(half, _FLAG_HALF, 0))
    i32 = lambda a: a.astype(jnp.int32)
    return i32(blk), i32(we), i32(wj), i32(oj), i32(fl)


def _moe(xn, logits, w_gate_up, b_gate_up, w_down, b_down):
    nt, d = xn.shape
    e_num, _, de2 = w_gate_up.shape
    de = de2 // 2
    rows = MOE_ROWS
    na = nt * TOP_K
    top_v, top_e = lax.top_k(logits, TOP_K)
    gate = jax.nn.softmax(top_v, axis=-1)
    flat_e = top_e.reshape(-1).astype(jnp.int32)
    order = jnp.argsort(flat_e).astype(jnp.int32)
    rank = jnp.argsort(order).astype(jnp.int32)
    counts = jnp.sum(flat_e[:, None] == jnp.arange(e_num, dtype=jnp.int32)[None, :], axis=0).astype(jnp.int32)
    nblk = (counts + rows - 1) // rows
    blk_end = jnp.cumsum(nblk)
    blk_start = blk_end - nblk
    sort_start = jnp.cumsum(counts) - counts
    nb_total = -(-na // rows) + e_num
    n_slots = nb_total * rows
    slot_of = blk_start[flat_e] * rows + rank - sort_start[flat_e]
    s = jnp.arange(n_slots, dtype=jnp.int32)
    e_s = jnp.minimum(jnp.sum((s // rows)[:, None] >= blk_end[None, :], axis=1), e_num - 1).astype(jnp.int32)
    pos = s - blk_start[e_s] * rows
    src = jnp.clip(sort_start[e_s] + pos, 0, na - 1)
    slot_tok = jnp.where(pos < counts[e_s], order[src] // TOP_K, s % nt)
    x_sorted = jnp.take(xn, slot_tok, axis=0, mode="clip")

    tn = _tile(de, 512, _LANES)
    nj = de // tn
    tabs = _work_tables(counts, rows, nb_total, nj)
    bgu = b_gate_up.reshape(e_num, 1, de2)
    act = pl.pallas_call(
        _moe_up_kernel,
        grid_spec=pltpu.PrefetchScalarGridSpec(
            num_scalar_prefetch=5,
            grid=(nj * nb_total,),
            in_specs=[
                pl.BlockSpec((rows, d), lambda w, blk, we, wj, oj, fl: (blk[w], 0)),
                pl.BlockSpec((None, d, tn), lambda w, blk, we, wj, oj, fl: (we[w], 0, wj[w])),
                pl.BlockSpec((None, d, tn), lambda w, blk, we, wj, oj, fl: (we[w], 0, nj + wj[w])),
                pl.BlockSpec((None, 1, tn), lambda w, blk, we, wj, oj, fl: (we[w], 0, wj[w])),
                pl.BlockSpec((None, 1, tn), lambda w, blk, we, wj, oj, fl: (we[w], 0, nj + wj[w])),
            ],
            out_specs=pl.BlockSpec((rows, tn), lambda w, blk, we, wj, oj, fl: (blk[w], oj[w])),
            scratch_shapes=[pltpu.VMEM((d, tn), _BF16), pltpu.VMEM((d, tn), _BF16)],
        ),
        out_shape=jax.ShapeDtypeStruct((n_slots, de), _BF16),
        compiler_params=_params("arbitrary"),
        name="moe_gate_up",
    )(*tabs, x_sorted, w_gate_up, w_gate_up, bgu, bgu)

    tn2 = _tile(d, 1024, _LANES)
    nj2 = d // tn2
    tabs2 = _work_tables(counts, rows, nb_total, nj2)
    y_sorted = pl.pallas_call(
        _moe_down_kernel,
        grid_spec=pltpu.PrefetchScalarGridSpec(
            num_scalar_prefetch=5,
            grid=(nj2 * nb_total,),
            in_specs=[
                pl.BlockSpec((rows, de), lambda w, blk, we, wj, oj, fl: (blk[w], 0)),
                pl.BlockSpec((None, de, tn2), lambda w, blk, we, wj, oj, fl: (we[w], 0, wj[w])),
                pl.BlockSpec((None, 1, tn2), lambda w, blk, we, wj, oj, fl: (we[w], 0, wj[w])),
            ],
            out_specs=pl.BlockSpec((rows, tn2), lambda w, blk, we, wj, oj, fl: (blk[w], oj[w])),
            scratch_shapes=[pltpu.VMEM((de, tn2), _BF16)],
        ),
        out_shape=jax.ShapeDtypeStruct((n_slots, d), _F32),
        compiler_params=_params("arbitrary"),
        name="moe_down",
    )(*tabs2, act, w_down, b_down.reshape(e_num, 1, d))

    idx = slot_of.reshape(nt, TOP_K).T.reshape(-1)
    yk = jnp.take(y_sorted, idx, axis=0, mode="clip").reshape(TOP_K, nt, d)
    return yk, gate


def kernel(x_prompt, x_sample, state_conv, state_mlstm_C, state_mlstm_n, state_mlstm_m, state_hgrn_S,
           g_mix, w_in, b_in, conv_w, conv_b, gain_a, gain_b, lb_logits, w_branch_a, w_branch_b, w_out,
           g_ffn, w_router, b_router, w_gate_up, b_gate_up, w_down, b_down, g_final):
    depth = w_in.shape[0]
    assert depth == 1, "single-layer step"
    bp, tp, d = x_prompt.shape
    bs, ts, _ = x_sample.shape
    nh_a, dk_a, dv_a = state_mlstm_C.shape[2:]
    nh_b, dh_b = state_hgrn_S.shape[2:4]
    qk2, va, wb = 2 * nh_a * dk_a, nh_a * dv_a, nh_b * dh_b
    assert qk2 == d and va == d and wb == d, "column blocks of the projection must have equal width"
    np_rows, ns_rows = bp * tp, bs * ts
    l = 0

    x = jnp.concatenate([x_prompt.reshape(np_rows, d), x_sample.reshape(ns_rows, d)], axis=0)

    sizes = (qk2, va, va, nh_a, nh_a, wb, wb, wb, wb, d, d)
    offs = np.concatenate([[0], np.cumsum(sizes)])
    big = [i for i, s in enumerate(sizes) if s == d]
    wt_l, b_l = jnp.swapaxes(w_in[l], 0, 1), b_in[l]
    wt_main = _pack_rows(wt_l, int(offs[3]), int(offs[5]), min(d, 512))
    b_main = jnp.concatenate([b_l[offs[i]:offs[i + 1]] for i in big])
    ng = 2 * nh_a
    wt_gate = jnp.pad(wt_l[offs[3]:offs[5], :], ((0, _LANES - ng), (0, 0))).astype(_BF16)
    b_gate = jnp.pad(b_l[offs[3]:offs[5]], (0, _LANES - ng))

    u = _rms_cast(x, g_mix[l], _BF16)
    proj = _mm_bias_nt(u, wt_main, b_main, _F32, tm_pref=832)
    gates = _mm_bias_nt(u, wt_gate, b_gate, _F32, tn_pref=_LANES)[:, :ng]

    lb = jnp.cumsum(jax.nn.softmax(lb_logits.astype(_F32), axis=0), axis=0)[l]

    def mixers(row0, B, T, conv0, c0, n0, m0, s0):
        ya, cv, cn, nn, mn = _mlstm(proj, gates, row0, B, T, conv0, c0, n0, m0,
                                    conv_w[l], conv_b[l], gain_a[l], cols=(0, 1, 2))
        yb, sn = _hgrn(proj, row0, B, T, s0, lb, gain_b[l], cols=(3, 4, 5, 6))
        return ya, yb, (cv, cn, nn, mn, sn)

    zeros = lambda *s: jnp.zeros(s, _F32)
    ya_p, yb_p, st_p = mixers(0, bp, tp, zeros(bp, conv_w.shape[1] - 1, qk2), zeros(bp, nh_a, dk_a, dv_a),
                              zeros(bp, nh_a, dk_a), zeros(bp, nh_a), zeros(bp, nh_b, dh_b, dh_b))
    ya_s, yb_s, st_s = mixers(np_rows, bs, ts, state_conv[l], state_mlstm_C[l], state_mlstm_n[l],
                              state_mlstm_m[l], state_hgrn_S[l])
    ya = jnp.concatenate([ya_p, ya_s], axis=0)
    yb = jnp.concatenate([yb_p, yb_s], axis=0)

    z = _merge(ya, yb, w_branch_a[l].astype(_BF16), w_branch_b[l].astype(_BF16), proj, 7, 8)
    h = _outproj(z, w_out[l].astype(_BF16), x)

    xn, logits = _norm_router(h, g_ffn[l], w_router[l], b_router[l])
    yk, gate = _moe(xn, logits, w_gate_up[l], b_gate_up[l], w_down[l], b_down[l])
    out_p, out_s = _final_norm(h, yk, gate, g_final, np_rows, ns_rows)

    states = [s[None] for s in st_p] + [s[None] for s in st_s]
    return (out_p.reshape(bp, tp, d), out_s.reshape(bs, ts, d), *states)
```

```python
import functools
import math

import numpy as np
import jax
import jax.numpy as jnp
from jax import lax
from jax.experimental import pallas as pl
from jax.experimental.pallas import tpu as pltpu

_F32 = jnp.float32
_BF16 = jnp.bfloat16

EPS = 1e-6
TOP_K = 4
CHUNK = 64
SWIGLU_LIMIT = 7.0
SWIGLU_ALPHA = 1.702
MOE_ROWS = 512
HGRN_HEADS_PER_STEP = 8

_VMEM_LIMIT_BYTES = 56 * 1024 * 1024
_LANES = 128

_NT = (((1,), (1,)), ((), ()))
_TN = (((0,), (0,)), ((), ()))


def _params(*sem):
    return pltpu.CompilerParams(dimension_semantics=sem, vmem_limit_bytes=_VMEM_LIMIT_BYTES)


def _tile(n, pref, mult):
    best = None
    for t in range(mult, min(n, pref) + 1, mult):
        if n % t == 0:
            best = t
    return best if best is not None else n


def _sigmoid(x):
    return 1.0 / (1.0 + jnp.exp(-x))


def _dot(a, b):
    return jnp.dot(a, b, preferred_element_type=_F32)


def _rms_cast_kernel(x_ref, g_ref, o_ref):
    x = x_ref[...]
    y = x * lax.rsqrt(jnp.mean(x * x, axis=-1, keepdims=True) + EPS) * g_ref[...]
    o_ref[...] = y.astype(o_ref.dtype)


def _rms_cast(x, g, out_dtype):
    n, d = x.shape
    tm = _tile(n, 512, 8)
    return pl.pallas_call(
        _rms_cast_kernel,
        grid=(n // tm,),
        in_specs=[pl.BlockSpec((tm, d), lambda i: (i, 0)), pl.BlockSpec((1, d), lambda i: (0, 0))],
        out_specs=pl.BlockSpec((tm, d), lambda i: (i, 0)),
        out_shape=jax.ShapeDtypeStruct((n, d), out_dtype),
        compiler_params=_params("parallel"),
        name="rms_cast",
    )(x, g.reshape(1, d))


def _pack_rows_kernel(a_ref, b_ref, o_ref, *, shift, first_shifted):
    i = pl.program_id(0)

    @pl.when(i < first_shifted)
    def _():
        o_ref[...] = a_ref[...].astype(o_ref.dtype)

    @pl.when(i >= first_shifted)
    def _():
        tb = o_ref.shape[0]
        o_ref[0:tb - shift, :] = a_ref[shift:tb, :].astype(o_ref.dtype)
        o_ref[tb - shift:tb, :] = b_ref[...].astype(o_ref.dtype)


def _pack_rows(wt, cut0, cut1, tb):
    n, k = wt.shape
    shift = cut1 - cut0
    n_out = n - shift
    assert cut0 % tb == 0 and n_out % tb == 0 and shift % 16 == 0 and tb % shift == 0
    return pl.pallas_call(
        functools.partial(_pack_rows_kernel, shift=shift, first_shifted=cut0 // tb),
        grid=(n_out // tb,),
        in_specs=[
            pl.BlockSpec((tb, k), lambda i: (i, 0)),
            pl.BlockSpec((shift, k), lambda i: ((i + 1) * (tb // shift), 0)),
        ],
        out_specs=pl.BlockSpec((tb, k), lambda i: (i, 0)),
        out_shape=jax.ShapeDtypeStruct((n_out, k), _BF16),
        compiler_params=_params("parallel"),
        name="pack_w_in",
    )(wt, wt)


def _mm_nt_kernel(x_ref, wt_ref, b_ref, o_ref):
    acc = lax.dot_general(x_ref[...], wt_ref[...], _NT, preferred_element_type=_F32)
    o_ref[...] = (acc + b_ref[...]).astype(o_ref.dtype)


def _mm_bias_nt(x, wt, b, out_dtype, tm_pref=640, tn_pref=1024):
    m, k = x.shape
    n = wt.shape[0]
    tm = _tile(m, tm_pref, 8)
    tn = _tile(n, tn_pref, _LANES)
    return pl.pallas_call(
        _mm_nt_kernel,
        grid=(m // tm, n // tn),
        in_specs=[
            pl.BlockSpec((tm, k), lambda i, j: (i, 0)),
            pl.BlockSpec((tn, k), lambda i, j: (j, 0)),
            pl.BlockSpec((1, tn), lambda i, j: (0, j)),
        ],
        out_specs=pl.BlockSpec((tm, tn), lambda i, j: (i, j)),
        out_shape=jax.ShapeDtypeStruct((m, n), out_dtype),
        compiler_params=_params("parallel", "parallel"),
        name="mm_bias",
    )(x, wt, b.reshape(1, n))


def _merge_kernel(ya_ref, yb_ref, wa_ref, wb_ref, ga_ref, gb_ref, o_ref):
    a = _dot(ya_ref[...], wa_ref[...])
    b = _dot(yb_ref[...], wb_ref[...])
    z = _sigmoid(ga_ref[...]) * a + _sigmoid(gb_ref[...]) * b
    o_ref[...] = z.astype(o_ref.dtype)


def _merge(ya, yb, wa, wb, proj, ga_col, gb_col):
    m, k = ya.shape
    n = wa.shape[1]
    tm = _tile(m, 640, 8)
    tn = _tile(n, 512, _LANES)
    nj = n // tn
    return pl.pallas_call(
        _merge_kernel,
        grid=(m // tm, nj),
        in_specs=[
            pl.BlockSpec((tm, k), lambda i, j: (i, 0)),
            pl.BlockSpec((tm, k), lambda i, j: (i, 0)),
            pl.BlockSpec((k, tn), lambda i, j: (0, j)),
            pl.BlockSpec((k, tn), lambda i, j: (0, j)),
            pl.BlockSpec((tm, tn), lambda i, j: (i, ga_col * nj + j)),
            pl.BlockSpec((tm, tn), lambda i, j: (i, gb_col * nj + j)),
        ],
        out_specs=pl.BlockSpec((tm, tn), lambda i, j: (i, j)),
        out_shape=jax.ShapeDtypeStruct((m, n), _BF16),
        compiler_params=_params("parallel", "parallel"),
        name="branch_merge",
    )(ya, yb, wa, wb, proj, proj)


def _outproj_kernel(z_ref, w_ref, x_ref, o_ref):
    o_ref[...] = x_ref[...] + _dot(z_ref[...], w_ref[...])


def _outproj(z, w, x):
    m, k = z.shape
    n = w.shape[1]
    tm = _tile(m, 640, 8)
    tn = _tile(n, 1024, _LANES)
    return pl.pallas_call(
        _outproj_kernel,
        grid=(m // tm, n // tn),
        in_specs=[
            pl.BlockSpec((tm, k), lambda i, j: (i, 0)),
            pl.BlockSpec((k, tn), lambda i, j: (0, j)),
            pl.BlockSpec((tm, tn), lambda i, j: (i, j)),
        ],
        out_specs=pl.BlockSpec((tm, tn), lambda i, j: (i, j)),
        out_shape=jax.ShapeDtypeStruct((m, n), _F32),
        compiler_params=_params("parallel", "parallel"),
        name="out_proj_residual",
    )(z, w, x)


def _split_bf16(x):
    hi = x.astype(_BF16)
    lo = (x - hi.astype(_F32)).astype(_BF16)
    return hi, lo


def _norm_router_kernel(h_ref, g_ref, wr_ref, br_ref, xn_ref, lg_ref):
    h = h_ref[...]
    xn = h * lax.rsqrt(jnp.mean(h * h, axis=-1, keepdims=True) + EPS) * g_ref[...]
    xn_ref[...] = xn.astype(xn_ref.dtype)
    xh, xl = _split_bf16(xn)
    wh, wl = _split_bf16(wr_ref[...])
    lg_ref[...] = _dot(xh, wh) + _dot(xl, wh) + _dot(xh, wl) + br_ref[...]


def _norm_router(h, g, w_router, b_router):
    n, d = h.shape
    e = w_router.shape[1]
    ep = -(-e // _LANES) * _LANES
    wr = jnp.pad(w_router.astype(_F32), ((0, 0), (0, ep - e)))
    br = jnp.pad(b_router.astype(_F32), (0, ep - e)).reshape(1, ep)
    tm = _tile(n, 512, 8)
    xn, lg = pl.pallas_call(
        _norm_router_kernel,
        grid=(n // tm,),
        in_specs=[
            pl.BlockSpec((tm, d), lambda i: (i, 0)),
            pl.BlockSpec((1, d), lambda i: (0, 0)),
            pl.BlockSpec((d, ep), lambda i: (0, 0)),
            pl.BlockSpec((1, ep), lambda i: (0, 0)),
        ],
        out_specs=[pl.BlockSpec((tm, d), lambda i: (i, 0)), pl.BlockSpec((tm, ep), lambda i: (i, 0))],
        out_shape=[jax.ShapeDtypeStruct((n, d), _BF16), jax.ShapeDtypeStruct((n, ep), _F32)],
        compiler_params=_params("parallel"),
        name="ffn_norm_router",
    )(h, g.reshape(1, d), wr, br)
    return xn, lg[:, :e]


def _final_kernel(h_ref, yk_ref, gate_ref, g_ref, op_ref, os_ref, *, n_prompt_tiles):
    i = pl.program_id(0)
    h = h_ref[...]
    for k in range(TOP_K):
        h = h + gate_ref[:, k:k + 1] * yk_ref[k]
    out = h * lax.rsqrt(jnp.mean(h * h, axis=-1, keepdims=True) + EPS) * g_ref[...]

    @pl.when(i < n_prompt_tiles)
    def _():
        op_ref[...] = out

    @pl.when(i >= n_prompt_tiles)
    def _():
        os_ref[...] = out


def _final_norm(h, yk, gate, g, n_prompt, n_sample):
    n, d = h.shape
    tm = _tile(math.gcd(n_prompt, n_sample), 128, 8)
    npt = n_prompt // tm
    return pl.pallas_call(
        functools.partial(_final_kernel, n_prompt_tiles=npt),
        grid=(n // tm,),
        in_specs=[
            pl.BlockSpec((tm, d), lambda i: (i, 0)),
            pl.BlockSpec((TOP_K, tm, d), lambda i: (0, i, 0)),
            pl.BlockSpec((tm, TOP_K), lambda i: (i, 0)),
            pl.BlockSpec((1, d), lambda i: (0, 0)),
        ],
        out_specs=[
            pl.BlockSpec((tm, d), lambda i: (jnp.minimum(i, npt - 1), 0)),
            pl.BlockSpec((tm, d), lambda i: (jnp.maximum(i - npt, 0), 0)),
        ],
        out_shape=[jax.ShapeDtypeStruct((n_prompt, d), _F32), jax.ShapeDtypeStruct((n_sample, d), _F32)],
        compiler_params=_params("arbitrary"),
        name="moe_combine_final_norm",
    )(h, yk, gate, g.reshape(1, d))


def _mlstm_kernel(qk_ref, v_ref, oa_ref, gcol_ref, grow_ref, cw_ref, cb_ref, gain_ref,
                  conv0_ref, c0_ref, n0_ref, m0_ref,
                  y_ref, conv_ref, c_ref, n_ref, m_ref,
                  xp_sc, qk_sc, *, nh, dk, dv, L, cw):
    c = pl.program_id(1)
    last = pl.num_programs(1) - 1
    pad = 8
    hist = cw - 1

    @pl.when(c == 0)
    def _():
        c_ref[...] = c0_ref[...]
        n_ref[...] = n0_ref[...]
        m_ref[...] = m0_ref[...]
        xp_sc[pad - hist:pad, :] = conv0_ref[0]

    @pl.when(c > 0)
    def _():
        xp_sc[pad - hist:pad, :] = xp_sc[pad + L - hist:pad + L, :]

    xp_sc[pad:pad + L, :] = qk_ref[...]

    @pl.when(c == last)
    def _():
        conv_ref[0] = xp_sc[pad + L - hist:pad + L, :]

    acc = cb_ref[...] + xp_sc[pad - hist:pad - hist + L, :] * cw_ref[0:1, :]
    for j in range(1, cw):
        acc = acc + xp_sc[pad - hist + j:pad - hist + j + L, :] * cw_ref[j:j + 1, :]
    qk_sc[...] = acc * _sigmoid(acc)

    row = lax.broadcasted_iota(jnp.int32, (L, L), 0)
    col = lax.broadcasted_iota(jnp.int32, (L, L), 1)
    causal = col <= row
    qk_off = nh * dk

    pending = []
    for h in range(nh):
        q = qk_sc[:, h * dk:(h + 1) * dk]
        k = qk_sc[:, qk_off + h * dk:qk_off + (h + 1) * dk] * (dk ** -0.5)
        v = v_ref[:, h * dv:(h + 1) * dv]
        f_col = gcol_ref[:, h:h + 1]
        i_col = gcol_ref[:, nh + h:nh + h + 1]
        f_row = grow_ref[0, h:h + 1, :]
        i_row = grow_ref[0, nh + h:nh + h + 1, :]
        m_prev = m_ref[0, :, h:h + 1]
        cmat = c_ref[0, h]
        nvec = n_ref[0, h:h + 1, :]

        d = jnp.where(causal, f_col - f_row + i_row, -jnp.inf)
        g = f_col + m_prev
        m_t = jnp.maximum(g, jnp.max(d, axis=1, keepdims=True))
        w = jnp.exp(d - m_t)
        a = jnp.exp(g - m_t)
        qb = q.astype(_BF16)
        vb = v.astype(_BF16)
        s = lax.dot_general(qb, k.astype(_BF16), _NT, preferred_element_type=_F32) * w
        num = a * _dot(qb, cmat.astype(_BF16)) + _dot(s.astype(_BF16), vb)
        den = a * jnp.sum(q * nvec, axis=1, keepdims=True) + jnp.sum(s, axis=1, keepdims=True)
        hh = num / jnp.maximum(jnp.abs(den), jnp.exp(-m_t))

        m_last = m_t[L - 1:L, :]
        a_last = a[L - 1:L, :]
        w_last = jnp.exp(f_col[L - 1:L, :] - f_col + i_col - m_last)
        kw = k * w_last
        yn = hh * lax.rsqrt(jnp.mean(hh * hh, axis=1, keepdims=True) + EPS)
        sl = slice(h * dv, (h + 1) * dv)
        pending.append((
            (yn * gain_ref[:, sl] * _sigmoid(oa_ref[:, sl])).astype(y_ref.dtype),
            lax.dot_general(kw.astype(_BF16), vb, _TN, preferred_element_type=_F32),
            a_last, a_last * nvec + jnp.sum(kw, axis=0, keepdims=True), m_last))

    for h, (y_h, kv_h, a_last, n_new, m_last) in enumerate(pending):
        y_ref[:, h * dv:(h + 1) * dv] = y_h
        c_ref[0, h] = a_last * c_ref[0, h] + kv_h
        n_ref[0, h:h + 1, :] = n_new
        m_ref[0, :, h:h + 1] = m_last


def _mlstm(proj, gates, row0, B, T, conv0, c0, n0, m0, conv_w, conv_b, gain_a, cols):
    nh, dk, dv = c0.shape[1], c0.shape[2], c0.shape[3]
    cw = conv_w.shape[0]
    L = min(T, CHUNK)
    nc = T // L
    qk2, va = 2 * nh * dk, nh * dv
    rb0 = row0 // L
    gt = lax.dynamic_slice_in_dim(gates, row0, B * T, axis=0)
    ig = gt[:, :nh].reshape(B, nc, L, nh)
    fcum = jnp.cumsum(jax.nn.log_sigmoid(gt[:, nh:2 * nh]).reshape(B, nc, L, nh), axis=2)
    both = jnp.concatenate([fcum, ig], axis=-1)
    gcol = both.reshape(B * T, 2 * nh)
    grow = jnp.transpose(both, (0, 1, 3, 2)).reshape(B * nc, 2 * nh, L)

    qk_cb, v_cb, oa_cb = cols
    kern = functools.partial(_mlstm_kernel, nh=nh, dk=dk, dv=dv, L=L, cw=cw)
    row_map = lambda cb: (lambda b, c: (rb0 + b * nc + c, cb))
    outs = pl.pallas_call(
        kern,
        grid=(B, nc),
        in_specs=[
            pl.BlockSpec((L, qk2), row_map(qk_cb)),
            pl.BlockSpec((L, va), row_map(v_cb)),
            pl.BlockSpec((L, va), row_map(oa_cb)),
            pl.BlockSpec((L, 2 * nh), lambda b, c: (b * nc + c, 0)),
            pl.BlockSpec((1, 2 * nh, L), lambda b, c: (b * nc + c, 0, 0)),
            pl.BlockSpec((cw, qk2), lambda b, c: (0, 0)),
            pl.BlockSpec((1, qk2), lambda b, c: (0, 0)),
            pl.BlockSpec((1, va), lambda b, c: (0, 0)),
            pl.BlockSpec((1, cw - 1, qk2), lambda b, c: (b, 0, 0)),
            pl.BlockSpec((1, nh, dk, dv), lambda b, c: (b, 0, 0, 0)),
            pl.BlockSpec((1, nh, dk), lambda b, c: (b, 0, 0)),
            pl.BlockSpec((1, 1, nh), lambda b, c: (b, 0, 0)),
        ],
        out_specs=[
            pl.BlockSpec((L, va), lambda b, c: (b * nc + c, 0)),
            pl.BlockSpec((1, cw - 1, qk2), lambda b, c: (b, 0, 0)),
            pl.BlockSpec((1, nh, dk, dv), lambda b, c: (b, 0, 0, 0)),
            pl.BlockSpec((1, nh, dk), lambda b, c: (b, 0, 0)),
            pl.BlockSpec((1, 1, nh), lambda b, c: (b, 0, 0)),
        ],
        out_shape=[
            jax.ShapeDtypeStruct((B * T, va), _BF16),
            jax.ShapeDtypeStruct((B, cw - 1, qk2), _F32),
            jax.ShapeDtypeStruct((B, nh, dk, dv), _F32),
            jax.ShapeDtypeStruct((B, nh, dk), _F32),
            jax.ShapeDtypeStruct((B, 1, nh), _F32),
        ],
        scratch_shapes=[pltpu.VMEM((L + 8, qk2), _F32), pltpu.VMEM((L, qk2), _F32)],
        compiler_params=_params("parallel", "arbitrary"),
        name="mlstm_chunks",
    )(proj, proj, proj, gcol, grow, conv_w, conv_b.reshape(1, qk2), gain_a.reshape(1, va),
      conv0, c0, n0, m0.reshape(B, 1, nh))
    y, conv_new, c_new, n_new, m_new = outs
    return y, conv_new, c_new, n_new, m_new.reshape(B, nh)


def _level_reference_rows(b_sc, l, L, width):
    half = 1 << l
    blk = 2 * half
    if blk >= 8:
        pieces = [jnp.broadcast_to(b_sc[i * blk + half - 1:i * blk + half, :], (blk, width))
                  for i in range(L // blk)]
    else:
        sub = lax.broadcasted_iota(jnp.int32, (8, width), 0)
        pieces = []
        for t8 in range(L // 8):
            piece = None
            for i in range(8 // blk):
                r = t8 * 8 + i * blk + half - 1
                cand = jnp.broadcast_to(b_sc[r:r + 1, :], (8, width))
                piece = cand if piece is None else jnp.where(sub >= i * blk, cand, piece)
            pieces.append(piece)
    return jnp.concatenate(pieces, axis=0)


def _hgrn_kernel(hf_ref, hi_ref, hq_ref, hg_ref, lb_ref, gain_ref, tril_ref, s0_ref,
                 y_ref, s_ref, st_sc, e_sc, b_sc, *, hpg, dh, L, nlev):
    c = pl.program_id(2)
    last = pl.num_programs(2) - 1
    width = hpg * dh

    @pl.when(c == 0)
    def _():
        for h in range(hpg):
            st_sc[h] = s0_ref[0, h].T

    lb = lb_ref[...]
    f = lb + (1.0 - lb) * _sigmoid(hf_ref[...])
    lf = jnp.log(f)
    kk = 1.0 - f
    hq = hq_ref[...]
    qq = hq * _sigmoid(hq)
    p0 = lf.astype(_BF16)
    r1 = lf - p0.astype(_F32)
    p1 = r1.astype(_BF16)
    p2 = (r1 - p1.astype(_F32)).astype(_BF16)
    tril = tril_ref[...]
    b = _dot(tril, p0) + _dot(tril, p1) + _dot(tril, p2)
    b_sc[...] = b
    e_sc[0:L, :] = jnp.exp(b)
    e_sc[L:2 * L, :] = jnp.exp(b_sc[L - 1:L, :] - b)
    rowid = lax.broadcasted_iota(jnp.int32, (L, width), 0)
    e_sc[2 * L:3 * L, :] = jnp.where((rowid & 1) == 1, f, 1.0)
    for l in range(1, nlev):
        delta = b - _level_reference_rows(b_sc, l, L, width)
        upper = ((rowid >> l) & 1) == 1
        e_sc[(2 + l) * L:(3 + l) * L, :] = jnp.exp(jnp.where(upper, delta, -delta))

    row = lax.broadcasted_iota(jnp.int32, (L, L), 0)
    col = lax.broadcasted_iota(jnp.int32, (L, L), 1)
    masks = []
    for l in range(nlev):
        same = (row >> (l + 1)) == (col >> (l + 1))
        masks.append(same & (((row >> l) & 1) == 1) & (((col >> l) & 1) == 0))
    diag = row == col

    sls = [slice(h * dh, (h + 1) * dh) for h in range(hpg)]
    ibs = [hi_ref[:, sl].astype(_BF16) for sl in sls]
    amats = []
    for sl in sls:
        q = qq[:, sl]
        k = kk[:, sl]
        amat = jnp.where(diag, jnp.sum(q * k, axis=1, keepdims=True), 0.0)
        for l in range(nlev):
            e = e_sc[(2 + l) * L:(3 + l) * L, sl]
            p = lax.dot_general((q * e).astype(_BF16), (k * e).astype(_BF16), _NT,
                                preferred_element_type=_F32)
            amat = jnp.where(masks[l], p, amat)
        amats.append(amat.astype(_BF16))
    outs, new_states = [], []
    for h, sl in enumerate(sls):
        e_cum = e_sc[0:L, sl]
        e_rev = e_sc[L:2 * L, sl]
        st = st_sc[h]
        outs.append(lax.dot_general((qq[:, sl] * e_cum).astype(_BF16), st.astype(_BF16), _NT,
                                    preferred_element_type=_F32)
                    + _dot(amats[h], ibs[h]))
        new_states.append(st * e_cum[L - 1:L, :]
                          + lax.dot_general(ibs[h], (kk[:, sl] * e_rev).astype(_BF16), _TN,
                                            preferred_element_type=_F32))
    for h, sl in enumerate(sls):
        st_sc[h] = new_states[h]
        o = outs[h]
        yn = o * lax.rsqrt(jnp.mean(o * o, axis=1, keepdims=True) + EPS)
        hg = hg_ref[:, sl]
        y_ref[:, sl] = (yn * gain_ref[:, sl] * (hg * _sigmoid(hg))).astype(y_ref.dtype)

    @pl.when(c == last)
    def _():
        for h in range(hpg):
            s_ref[0, h] = st_sc[h].T


def _hgrn(proj, row0, B, T, s0, lb, gain_b, cols):
    nh, dh = s0.shape[1], s0.shape[2]
    wb = nh * dh
    L = min(T, CHUNK)
    nc = T // L
    rb0 = row0 // L
    hpg = min(HGRN_HEADS_PER_STEP, nh)
    ng = nh // hpg
    gw = hpg * dh
    nlev = int(math.log2(L))
    assert 1 << nlev == L and L % 8 == 0
    tril = jnp.asarray(np.tril(np.ones((L, L), np.float32)), dtype=_BF16)
    nrow = (nlev + 2) * L
    hf_cb, hi_cb, hq_cb, hg_cb = cols
    kern = functools.partial(_hgrn_kernel, hpg=hpg, dh=dh, L=L, nlev=nlev)
    row_map = lambda cb: (lambda b, g, c: (rb0 + b * nc + c, cb * ng + g))
    y, s_new = pl.pallas_call(
        kern,
        grid=(B, ng, nc),
        in_specs=[
            pl.BlockSpec((L, gw), row_map(hf_cb)),
            pl.BlockSpec((L, gw), row_map(hi_cb)),
            pl.BlockSpec((L, gw), row_map(hq_cb)),
            pl.BlockSpec((L, gw), row_map(hg_cb)),
            pl.BlockSpec((1, gw), lambda b, g, c: (0, g)),
            pl.BlockSpec((1, gw), lambda b, g, c: (0, g)),
            pl.BlockSpec((L, L), lambda b, g, c: (0, 0)),
            pl.BlockSpec((1, hpg, dh, dh), lambda b, g, c: (b, g, 0, 0)),
        ],
        out_specs=[
            pl.BlockSpec((L, gw), lambda b, g, c: (b * nc + c, g)),
            pl.BlockSpec((1, hpg, dh, dh), lambda b, g, c: (b, g, 0, 0)),
        ],
        out_shape=[
            jax.ShapeDtypeStruct((B * T, wb), _BF16),
            jax.ShapeDtypeStruct((B, nh, dh, dh), _F32),
        ],
        scratch_shapes=[pltpu.VMEM((hpg, dh, dh), _F32), pltpu.VMEM((nrow, gw), _F32),
                        pltpu.VMEM((L, gw), _F32)],
        compiler_params=_params("parallel", "parallel", "arbitrary"),
        name="hgrn2_chunks",
    )(proj, proj, proj, proj, lb.reshape(1, wb), gain_b.reshape(1, wb), tril, s0)
    return y, s_new


def _expert_rows_loop(b0, nb, rows, x_hbm, out_hbm, col0, ncol, xbuf, obuf, sem_in, sem_out, compute):
    def x_copy(r, slot):
        return pltpu.make_async_copy(x_hbm.at[pl.ds((b0 + r) * rows, rows), :], xbuf.at[slot], sem_in.at[slot])

    def o_copy(r, slot):
        return pltpu.make_async_copy(obuf.at[slot],
                                     out_hbm.at[pl.ds((b0 + r) * rows, rows), pl.ds(col0, ncol)],
                                     sem_out.at[slot])

    def body(r, carry):
        slot = lax.rem(r, 2)
        x_copy(r, slot).wait()

        @pl.when(r + 1 < nb)
        def _():
            x_copy(r + 1, 1 - slot).start()

        @pl.when(r >= 2)
        def _():
            o_copy(r - 2, slot).wait()

        obuf[slot] = compute(xbuf[slot])
        o_copy(r, slot).start()
        return carry

    lax.fori_loop(0, nb, body, 0)

    @pl.when(nb >= 2)
    def _():
        o_copy(nb - 2, lax.rem(nb, 2)).wait()

    o_copy(nb - 1, lax.rem(nb - 1, 2)).wait()


def _zero_row_blocks(b_from, b_to, rows, out_hbm, col0, ncol, obuf, sem_out):
    obuf[0] = jnp.zeros(obuf.shape[1:], obuf.dtype)

    def body(t, carry):
        cp = pltpu.make_async_copy(obuf.at[0], out_hbm.at[pl.ds(t * rows, rows), pl.ds(col0, ncol)],
                                   sem_out.at[0])
        cp.start()
        cp.wait()
        return carry

    lax.fori_loop(b_from, b_to, body, 0)


def _moe_up_kernel(bs_ref, nb_ref, x_hbm, wg_ref, wu_ref, bg_ref, bu_ref, act_hbm,
                   wg_sc, wu_sc, xbuf, obuf, sem_in, sem_out, *, nj, rows, tn, n_exp, nb_total):
    g = pl.program_id(0)
    e = g // nj
    col0 = pl.multiple_of((g % nj) * tn, tn)
    b0 = bs_ref[e]
    nb = nb_ref[e]

    def compute(x):
        gt = jnp.minimum(_dot(x, wg_sc[...]) + bg_ref[...], SWIGLU_LIMIT)
        up = jnp.clip(_dot(x, wu_sc[...]) + bu_ref[...], -SWIGLU_LIMIT, SWIGLU_LIMIT)
        return ((up + 1.0) * gt * _sigmoid(gt * SWIGLU_ALPHA)).astype(obuf.dtype)

    @pl.when(nb > 0)
    def _():
        pltpu.make_async_copy(x_hbm.at[pl.ds(b0 * rows, rows), :], xbuf.at[0], sem_in.at[0]).start()
        wg_sc[...] = wg_ref[...].astype(_BF16)
        wu_sc[...] = wu_ref[...].astype(_BF16)
        _expert_rows_loop(b0, nb, rows, x_hbm, act_hbm, col0, tn, xbuf, obuf, sem_in, sem_out, compute)

    @pl.when(e == n_exp - 1)
    def _():
        _zero_row_blocks(b0 + nb, nb_total, rows, act_hbm, col0, tn, obuf, sem_out)


def _moe_down_kernel(bs_ref, nb_ref, x_hbm, wd_ref, bd_ref, y_hbm,
                     wd_sc, xbuf, obuf, sem_in, sem_out, *, nj, rows, tn, n_exp, nb_total):
    g = pl.program_id(0)
    e = g // nj
    col0 = pl.multiple_of((g % nj) * tn, tn)
    b0 = bs_ref[e]
    nb = nb_ref[e]

    def compute(x):
        return _dot(x, wd_sc[...]) + bd_ref[...]

    @pl.when(nb > 0)
    def _():
        pltpu.make_async_copy(x_hbm.at[pl.ds(b0 * rows, rows), :], xbuf.at[0], sem_in.at[0]).start()
        wd_sc[...] = wd_ref[...].astype(_BF16)
        _expert_rows_loop(b0, nb, rows, x_hbm, y_hbm, col0, tn, xbuf, obuf, sem_in, sem_out, compute)

    @pl.when(e == n_exp - 1)
    def _():
        _zero_row_blocks(b0 + nb, nb_total, rows, y_hbm, col0, tn, obuf, sem_out)


def _moe(xn, logits, w_gate_up, b_gate_up, w_down, b_down):
    nt, d = xn.shape
    e_num, _, de2 = w_gate_up.shape
    de = de2 // 2
    rows = MOE_ROWS
    na = nt * TOP_K
    top_v, top_e = lax.top_k(logits, TOP_K)
    gate = jax.nn.softmax(top_v, axis=-1)
    flat_e = top_e.reshape(-1).astype(jnp.int32)
    order = jnp.argsort(flat_e).astype(jnp.int32)
    rank = jnp.argsort(order).astype(jnp.int32)
    counts = jnp.sum(flat_e[:, None] == jnp.arange(e_num, dtype=jnp.int32)[None, :], axis=0).astype(jnp.int32)
    nblk = (counts + rows - 1) // rows
    blk_end = jnp.cumsum(nblk)
    blk_start = blk_end - nblk
    sort_start = jnp.cumsum(counts) - counts
    nb_total = -(-na // rows) + e_num
    n_slots = nb_total * rows
    slot_of = blk_start[flat_e] * rows + rank - sort_start[flat_e]
    s = jnp.arange(n_slots, dtype=jnp.int32)
    e_s = jnp.minimum(jnp.sum((s // rows)[:, None] >= blk_end[None, :], axis=1), e_num - 1).astype(jnp.int32)
    pos = s - blk_start[e_s] * rows
    src = jnp.clip(sort_start[e_s] + pos, 0, na - 1)
    slot_tok = jnp.where(pos < counts[e_s], order[src] // TOP_K, s % nt)
    x_sorted = jnp.take(xn, slot_tok, axis=0, mode="clip")

    tn = _tile(de, 512, _LANES)
    nj = de // tn
    bgu = b_gate_up.reshape(e_num, 1, de2)
    hbm = pl.BlockSpec(memory_space=pl.ANY)
    dma2 = pltpu.SemaphoreType.DMA((2,))
    act = pl.pallas_call(
        functools.partial(_moe_up_kernel, nj=nj, rows=rows, tn=tn, n_exp=e_num, nb_total=nb_total),
        grid_spec=pltpu.PrefetchScalarGridSpec(
            num_scalar_prefetch=2,
            grid=(e_num * nj,),
            in_specs=[
                hbm,
                pl.BlockSpec((None, d, tn), lambda g, bs, nb: (g // nj, 0, g % nj)),
                pl.BlockSpec((None, d, tn), lambda g, bs, nb: (g // nj, 0, nj + g % nj)),
                pl.BlockSpec((None, 1, tn), lambda g, bs, nb: (g // nj, 0, g % nj)),
                pl.BlockSpec((None, 1, tn), lambda g, bs, nb: (g // nj, 0, nj + g % nj)),
            ],
            out_specs=hbm,
            scratch_shapes=[pltpu.VMEM((d, tn), _BF16), pltpu.VMEM((d, tn), _BF16),
                            pltpu.VMEM((2, rows, d), _BF16), pltpu.VMEM((2, rows, tn), _BF16), dma2, dma2],
        ),
        out_shape=jax.ShapeDtypeStruct((n_slots, de), _BF16),
        compiler_params=_params("arbitrary"),
        name="moe_gate_up",
    )(blk_start, nblk, x_sorted, w_gate_up, w_gate_up, bgu, bgu)

    tn2 = _tile(d, 1024, _LANES)
    nj2 = d // tn2
    y_sorted = pl.pallas_call(
        functools.partial(_moe_down_kernel, nj=nj2, rows=rows, tn=tn2, n_exp=e_num, nb_total=nb_total),
        grid_spec=pltpu.PrefetchScalarGridSpec(
            num_scalar_prefetch=2,
            grid=(e_num * nj2,),
            in_specs=[
                hbm,
                pl.BlockSpec((None, de, tn2), lambda g, bs, nb: (g // nj2, 0, g % nj2)),
                pl.BlockSpec((None, 1, tn2), lambda g, bs, nb: (g // nj2, 0, g % nj2)),
            ],
            out_specs=hbm,
            scratch_shapes=[pltpu.VMEM((de, tn2), _BF16),
                            pltpu.VMEM((2, rows, de), _BF16), pltpu.VMEM((2, rows, tn2), _F32), dma2, dma2],
        ),
        out_shape=jax.ShapeDtypeStruct((n_slots, d), _F32),
        compiler_params=_params("arbitrary"),
        name="moe_down",
    )(blk_start, nblk, act, w_down, b_down.reshape(e_num, 1, d))

    idx = slot_of.reshape(nt, TOP_K).T.reshape(-1)
    yk = jnp.take(y_sorted, idx, axis=0, mode="clip").reshape(TOP_K, nt, d)
    return yk, gate


def kernel(x_prompt, x_sample, state_conv, state_mlstm_C, state_mlstm_n, state_mlstm_m, state_hgrn_S,
           g_mix, w_in, b_in, conv_w, conv_b, gain_a, gain_b, lb_logits, w_branch_a, w_branch_b, w_out,
           g_ffn, w_router, b_router, w_gate_up, b_gate_up, w_down, b_down, g_final):
    depth = w_in.shape[0]
    assert depth == 1, "single-layer step"
    bp, tp, d = x_prompt.shape
    bs, ts, _ = x_sample.shape
    nh_a, dk_a, dv_a = state_mlstm_C.shape[2:]
    nh_b, dh_b = state_hgrn_S.shape[2:4]
    qk2, va, wb = 2 * nh_a * dk_a, nh_a * dv_a, nh_b * dh_b
    assert qk2 == d and va == d and wb == d, "column blocks of the projection must have equal width"
    np_rows, ns_rows = bp * tp, bs * ts
    l = 0

    x = jnp.concatenate([x_prompt.reshape(np_rows, d), x_sample.reshape(ns_rows, d)], axis=0)

    sizes = (qk2, va, va, nh_a, nh_a, wb, wb, wb, wb, d, d)
    offs = np.concatenate([[0], np.cumsum(sizes)])
    big = [i for i, s in enumerate(sizes) if s == d]
    wt_l, b_l = jnp.swapaxes(w_in[l], 0, 1), b_in[l]
    wt_main = _pack_rows(wt_l, int(offs[3]), int(offs[5]), min(d, 512))
    b_main = jnp.concatenate([b_l[offs[i]:offs[i + 1]] for i in big])
    ng = 2 * nh_a
    wt_gate = jnp.pad(wt_l[offs[3]:offs[5], :], ((0, _LANES - ng), (0, 0))).astype(_BF16)
    b_gate = jnp.pad(b_l[offs[3]:offs[5]], (0, _LANES - ng))

    u = _rms_cast(x, g_mix[l], _BF16)
    proj = _mm_bias_nt(u, wt_main, b_main, _F32, tm_pref=832)
    gates = _mm_bias_nt(u, wt_gate, b_gate, _F32, tn_pref=_LANES)[:, :ng]

    lb = jnp.cumsum(jax.nn.softmax(lb_logits.astype(_F32), axis=0), axis=0)[l]

    def mixers(row0, B, T, conv0, c0, n0, m0, s0):
        ya, cv, cn, nn, mn = _mlstm(proj, gates, row0, B, T, conv0, c0, n0, m0,
                                    conv_w[l], conv_b[l], gain_a[l], cols=(0, 1, 2))
        yb, sn = _hgrn(proj, row0, B, T, s0, lb, gain_b[l], cols=(3, 4, 5, 6))
        return ya, yb, (cv, cn, nn, mn, sn)

    zeros = lambda *s: jnp.zeros(s, _F32)
    ya_p, yb_p, st_p = mixers(0, bp, tp, zeros(bp, conv_w.shape[1] - 1, qk2), zeros(bp, nh_a, dk_a, dv_a),
                              zeros(bp, nh_a, dk_a), zeros(bp, nh_a), zeros(bp, nh_b, dh_b, dh_b))
    ya_s, yb_s, st_s = mixers(np_rows, bs, ts, state_conv[l], state_mlstm_C[l], state_mlstm_n[l],
                              state_mlstm_m[l], state_hgrn_S[l])
    ya = jnp.concatenate([ya_p, ya_s], axis=0)
    yb = jnp.concatenate([yb_p, yb_s], axis=0)

    z = _merge(ya, yb, w_branch_a[l].astype(_BF16), w_branch_b[l].astype(_BF16), proj, 7, 8)
    h = _outproj(z, w_out[l].astype(_BF16), x)

    xn, logits = _norm_router(h, g_ffn[l], w_router[l], b_router[l])
    yk, gate = _moe(xn, logits, w_gate_up[l], b_gate_up[l], w_down[l], b_down[l])
    out_p, out_s = _final_norm(h, yk, gate, g_final, np_rows, ns_rows)

    states = [s[None] for s in st_p] + [s[None] for s in st_s]
    return (out_p.reshape(bp, tp, d), out_s.reshape(bs, ts, d), *states)
```

```python
import functools
import math

import numpy as np
import jax
import jax.numpy as jnp
from jax import lax
from jax.experimental import pallas as pl
from jax.experimental.pallas import tpu as pltpu

_F32 = jnp.float32
_BF16 = jnp.bfloat16

EPS = 1e-6
TOP_K = 4
CHUNK = 64
SWIGLU_LIMIT = 7.0
SWIGLU_ALPHA = 1.702
MOE_ROWS = 512
HGRN_HEADS_PER_STEP = 8

_VMEM_LIMIT_BYTES = 56 * 1024 * 1024
_LANES = 128

_NT = (((1,), (1,)), ((), ()))
_TN = (((0,), (0,)), ((), ()))


def _params(*sem):
    return pltpu.CompilerParams(dimension_semantics=sem, vmem_limit_bytes=_VMEM_LIMIT_BYTES)


def _tile(n, pref, mult):
    best = None
    for t in range(mult, min(n, pref) + 1, mult):
        if n % t == 0:
            best = t
    return best if best is not None else n


def _sigmoid(x):
    return 1.0 / (1.0 + jnp.exp(-x))


def _dot(a, b):
    return jnp.dot(a, b, preferred_element_type=_F32)


def _rms_cast_kernel(x_ref, g_ref, o_ref):
    x = x_ref[...]
    y = x * lax.rsqrt(jnp.mean(x * x, axis=-1, keepdims=True) + EPS) * g_ref[...]
    o_ref[...] = y.astype(o_ref.dtype)


def _rms_cast(x, g, out_dtype):
    n, d = x.shape
    tm = _tile(n, 512, 8)
    return pl.pallas_call(
        _rms_cast_kernel,
        grid=(n // tm,),
        in_specs=[pl.BlockSpec((tm, d), lambda i: (i, 0)), pl.BlockSpec((1, d), lambda i: (0, 0))],
        out_specs=pl.BlockSpec((tm, d), lambda i: (i, 0)),
        out_shape=jax.ShapeDtypeStruct((n, d), out_dtype),
        compiler_params=_params("parallel"),
        name="rms_cast",
    )(x, g.reshape(1, d))


def _pack_rows_kernel(a_ref, b_ref, o_ref, *, shift, first_shifted):
    i = pl.program_id(0)

    @pl.when(i < first_shifted)
    def _():
        o_ref[...] = a_ref[...].astype(o_ref.dtype)

    @pl.when(i >= first_shifted)
    def _():
        tb = o_ref.shape[0]
        o_ref[0:tb - shift, :] = a_ref[shift:tb, :].astype(o_ref.dtype)
        o_ref[tb - shift:tb, :] = b_ref[...].astype(o_ref.dtype)


def _pack_rows(wt, cut0, cut1, tb):
    n, k = wt.shape
    shift = cut1 - cut0
    n_out = n - shift
    assert cut0 % tb == 0 and n_out % tb == 0 and shift % 16 == 0 and tb % shift == 0
    return pl.pallas_call(
        functools.partial(_pack_rows_kernel, shift=shift, first_shifted=cut0 // tb),
        grid=(n_out // tb,),
        in_specs=[
            pl.BlockSpec((tb, k), lambda i: (i, 0)),
            pl.BlockSpec((shift, k), lambda i: ((i + 1) * (tb // shift), 0)),
        ],
        out_specs=pl.BlockSpec((tb, k), lambda i: (i, 0)),
        out_shape=jax.ShapeDtypeStruct((n_out, k), _BF16),
        compiler_params=_params("parallel"),
        name="pack_w_in",
    )(wt, wt)


def _mm_nt_kernel(x_ref, wt_ref, b_ref, o_ref):
    acc = lax.dot_general(x_ref[...], wt_ref[...], _NT, preferred_element_type=_F32)
    o_ref[...] = (acc + b_ref[...]).astype(o_ref.dtype)


def _mm_bias_nt(x, wt, b, out_dtype, tm_pref=640, tn_pref=1024):
    m, k = x.shape
    n = wt.shape[0]
    tm = _tile(m, tm_pref, 8)
    tn = _tile(n, tn_pref, _LANES)
    return pl.pallas_call(
        _mm_nt_kernel,
        grid=(m // tm, n // tn),
        in_specs=[
            pl.BlockSpec((tm, k), lambda i, j: (i, 0)),
            pl.BlockSpec((tn, k), lambda i, j: (j, 0)),
            pl.BlockSpec((1, tn), lambda i, j: (0, j)),
        ],
        out_specs=pl.BlockSpec((tm, tn), lambda i, j: (i, j)),
        out_shape=jax.ShapeDtypeStruct((m, n), out_dtype),
        compiler_params=_params("parallel", "parallel"),
        name="mm_bias",
    )(x, wt, b.reshape(1, n))


def _merge_kernel(ya_ref, yb_ref, wa_ref, wb_ref, ga_ref, gb_ref, o_ref):
    a = _dot(ya_ref[...], wa_ref[...])
    b = _dot(yb_ref[...], wb_ref[...])
    z = _sigmoid(ga_ref[...]) * a + _sigmoid(gb_ref[...]) * b
    o_ref[...] = z.astype(o_ref.dtype)


def _merge(ya, yb, wa, wb, proj, ga_col, gb_col):
    m, k = ya.shape
    n = wa.shape[1]
    tm = _tile(m, 640, 8)
    tn = _tile(n, 512, _LANES)
    nj = n // tn
    return pl.pallas_call(
        _merge_kernel,
        grid=(m // tm, nj),
        in_specs=[
            pl.BlockSpec((tm, k), lambda i, j: (i, 0)),
            pl.BlockSpec((tm, k), lambda i, j: (i, 0)),
            pl.BlockSpec((k, tn), lambda i, j: (0, j)),
            pl.BlockSpec((k, tn), lambda i, j: (0, j)),
            pl.BlockSpec((tm, tn), lambda i, j: (i, ga_col * nj + j)),
            pl.BlockSpec((tm, tn), lambda i, j: (i, gb_col * nj + j)),
        ],
        out_specs=pl.BlockSpec((tm, tn), lambda i, j: (i, j)),
        out_shape=jax.ShapeDtypeStruct((m, n), _BF16),
        compiler_params=_params("parallel", "parallel"),
        name="branch_merge",
    )(ya, yb, wa, wb, proj, proj)


def _outproj_kernel(z_ref, w_ref, x_ref, o_ref):
    o_ref[...] = x_ref[...] + _dot(z_ref[...], w_ref[...])


def _outproj(z, w, x):
    m, k = z.shape
    n = w.shape[1]
    tm = _tile(m, 640, 8)
    tn = _tile(n, 1024, _LANES)
    return pl.pallas_call(
        _outproj_kernel,
        grid=(m // tm, n // tn),
        in_specs=[
            pl.BlockSpec((tm, k), lambda i, j: (i, 0)),
            pl.BlockSpec((k, tn), lambda i, j: (0, j)),
            pl.BlockSpec((tm, tn), lambda i, j: (i, j)),
        ],
        out_specs=pl.BlockSpec((tm, tn), lambda i, j: (i, j)),
        out_shape=jax.ShapeDtypeStruct((m, n), _F32),
        compiler_params=_params("parallel", "parallel"),
        name="out_proj_residual",
    )(z, w, x)


def _split_bf16(x):
    hi = x.astype(_BF16)
    lo = (x - hi.astype(_F32)).astype(_BF16)
    return hi, lo


def _norm_router_kernel(h_ref, g_ref, wr_ref, br_ref, xn_ref, lg_ref):
    h = h_ref[...]
    xn = h * lax.rsqrt(jnp.mean(h * h, axis=-1, keepdims=True) + EPS) * g_ref[...]
    xn_ref[...] = xn.astype(xn_ref.dtype)
    xh, xl = _split_bf16(xn)
    wh, wl = _split_bf16(wr_ref[...])
    lg_ref[...] = _dot(xh, wh) + _dot(xl, wh) + _dot(xh, wl) + br_ref[...]


def _norm_router(h, g, w_router, b_router):
    n, d = h.shape
    e = w_router.shape[1]
    ep = -(-e // _LANES) * _LANES
    wr = jnp.pad(w_router.astype(_F32), ((0, 0), (0, ep - e)))
    br = jnp.pad(b_router.astype(_F32), (0, ep - e)).reshape(1, ep)
    tm = _tile(n, 512, 8)
    xn, lg = pl.pallas_call(
        _norm_router_kernel,
        grid=(n // tm,),
        in_specs=[
            pl.BlockSpec((tm, d), lambda i: (i, 0)),
            pl.BlockSpec((1, d), lambda i: (0, 0)),
            pl.BlockSpec((d, ep), lambda i: (0, 0)),
            pl.BlockSpec((1, ep), lambda i: (0, 0)),
        ],
        out_specs=[pl.BlockSpec((tm, d), lambda i: (i, 0)), pl.BlockSpec((tm, ep), lambda i: (i, 0))],
        out_shape=[jax.ShapeDtypeStruct((n, d), _BF16), jax.ShapeDtypeStruct((n, ep), _F32)],
        compiler_params=_params("parallel"),
        name="ffn_norm_router",
    )(h, g.reshape(1, d), wr, br)
    return xn, lg[:, :e]


def _final_kernel(h_ref, yk_ref, gate_ref, g_ref, op_ref, os_ref, *, n_prompt_tiles):
    i = pl.program_id(0)
    h = h_ref[...]
    for k in range(TOP_K):
        h = h + gate_ref[:, k:k + 1] * yk_ref[k]
    out = h * lax.rsqrt(jnp.mean(h * h, axis=-1, keepdims=True) + EPS) * g_ref[...]

    @pl.when(i < n_prompt_tiles)
    def _():
        op_ref[...] = out

    @pl.when(i >= n_prompt_tiles)
    def _():
        os_ref[...] = out


def _final_norm(h, yk, gate, g, n_prompt, n_sample):
    n, d = h.shape
    tm = _tile(math.gcd(n_prompt, n_sample), 128, 8)
    npt = n_prompt // tm
    return pl.pallas_call(
        functools.partial(_final_kernel, n_prompt_tiles=npt),
        grid=(n // tm,),
        in_specs=[
            pl.BlockSpec((tm, d), lambda i: (i, 0)),
            pl.BlockSpec((TOP_K, tm, d), lambda i: (0, i, 0)),
            pl.BlockSpec((tm, TOP_K), lambda i: (i, 0)),
            pl.BlockSpec((1, d), lambda i: (0, 0)),
        ],
        out_specs=[
            pl.BlockSpec((tm, d), lambda i: (jnp.minimum(i, npt - 1), 0)),
            pl.BlockSpec((tm, d), lambda i: (jnp.maximum(i - npt, 0), 0)),
        ],
        out_shape=[jax.ShapeDtypeStruct((n_prompt, d), _F32), jax.ShapeDtypeStruct((n_sample, d), _F32)],
        compiler_params=_params("arbitrary"),
        name="moe_combine_final_norm",
    )(h, yk, gate, g.reshape(1, d))


def _mlstm_kernel(qk_ref, v_ref, oa_ref, gcol_ref, grow_ref, cw_ref, cb_ref, gain_ref,
                  conv0_ref, c0_ref, n0_ref, m0_ref,
                  y_ref, conv_ref, c_ref, n_ref, m_ref,
                  xp_sc, qk_sc, *, nh, dk, dv, L, cw):
    c = pl.program_id(1)
    last = pl.num_programs(1) - 1
    pad = 8
    hist = cw - 1

    @pl.when(c == 0)
    def _():
        c_ref[...] = c0_ref[...]
        n_ref[...] = n0_ref[...]
        m_ref[...] = m0_ref[...]
        xp_sc[pad - hist:pad, :] = conv0_ref[0]

    @pl.when(c > 0)
    def _():
        xp_sc[pad - hist:pad, :] = xp_sc[pad + L - hist:pad + L, :]

    xp_sc[pad:pad + L, :] = qk_ref[...]

    @pl.when(c == last)
    def _():
        conv_ref[0] = xp_sc[pad + L - hist:pad + L, :]

    acc = cb_ref[...] + xp_sc[pad - hist:pad - hist + L, :] * cw_ref[0:1, :]
    for j in range(1, cw):
        acc = acc + xp_sc[pad - hist + j:pad - hist + j + L, :] * cw_ref[j:j + 1, :]
    qk_sc[...] = acc * _sigmoid(acc)

    row = lax.broadcasted_iota(jnp.int32, (L, L), 0)
    col = lax.broadcasted_iota(jnp.int32, (L, L), 1)
    causal = col <= row
    qk_off = nh * dk

    pending = []
    for h in range(nh):
        q = qk_sc[:, h * dk:(h + 1) * dk]
        k = qk_sc[:, qk_off + h * dk:qk_off + (h + 1) * dk] * (dk ** -0.5)
        v = v_ref[:, h * dv:(h + 1) * dv]
        f_col = gcol_ref[:, h:h + 1]
        i_col = gcol_ref[:, nh + h:nh + h + 1]
        f_row = grow_ref[0, h:h + 1, :]
        i_row = grow_ref[0, nh + h:nh + h + 1, :]
        m_prev = m_ref[0, :, h:h + 1]
        cmat = c_ref[0, h]
        nvec = n_ref[0, h:h + 1, :]

        d = jnp.where(causal, f_col - f_row + i_row, -jnp.inf)
        g = f_col + m_prev
        m_t = jnp.maximum(g, jnp.max(d, axis=1, keepdims=True))
        w = jnp.exp(d - m_t)
        a = jnp.exp(g - m_t)
        qb = q.astype(_BF16)
        vb = v.astype(_BF16)
        s = lax.dot_general(qb, k.astype(_BF16), _NT, preferred_element_type=_F32) * w
        num = a * _dot(qb, cmat.astype(_BF16)) + _dot(s.astype(_BF16), vb)
        den = a * jnp.sum(q * nvec, axis=1, keepdims=True) + jnp.sum(s, axis=1, keepdims=True)
        hh = num / jnp.maximum(jnp.abs(den), jnp.exp(-m_t))

        m_last = m_t[L - 1:L, :]
        a_last = a[L - 1:L, :]
        w_last = jnp.exp(f_col[L - 1:L, :] - f_col + i_col - m_last)
        kw = k * w_last
        yn = hh * lax.rsqrt(jnp.mean(hh * hh, axis=1, keepdims=True) + EPS)
        sl = slice(h * dv, (h + 1) * dv)
        pending.append((
            (yn * gain_ref[:, sl] * _sigmoid(oa_ref[:, sl])).astype(y_ref.dtype),
            lax.dot_general(kw.astype(_BF16), vb, _TN, preferred_element_type=_F32),
            a_last, a_last * nvec + jnp.sum(kw, axis=0, keepdims=True), m_last))

    for h, (y_h, kv_h, a_last, n_new, m_last) in enumerate(pending):
        y_ref[:, h * dv:(h + 1) * dv] = y_h
        c_ref[0, h] = a_last * c_ref[0, h] + kv_h
        n_ref[0, h:h + 1, :] = n_new
        m_ref[0, :, h:h + 1] = m_last


def _mlstm(proj, gates, row0, B, T, conv0, c0, n0, m0, conv_w, conv_b, gain_a, cols):
    nh, dk, dv = c0.shape[1], c0.shape[2], c0.shape[3]
    cw = conv_w.shape[0]
    L = min(T, CHUNK)
    nc = T // L
    qk2, va = 2 * nh * dk, nh * dv
    rb0 = row0 // L
    gt = lax.dynamic_slice_in_dim(gates, row0, B * T, axis=0)
    ig = gt[:, :nh].reshape(B, nc, L, nh)
    fcum = jnp.cumsum(jax.nn.log_sigmoid(gt[:, nh:2 * nh]).reshape(B, nc, L, nh), axis=2)
    both = jnp.concatenate([fcum, ig], axis=-1)
    gcol = both.reshape(B * T, 2 * nh)
    grow = jnp.transpose(both, (0, 1, 3, 2)).reshape(B * nc, 2 * nh, L)

    qk_cb, v_cb, oa_cb = cols
    kern = functools.partial(_mlstm_kernel, nh=nh, dk=dk, dv=dv, L=L, cw=cw)
    row_map = lambda cb: (lambda b, c: (rb0 + b * nc + c, cb))
    outs = pl.pallas_call(
        kern,
        grid=(B, nc),
        in_specs=[
            pl.BlockSpec((L, qk2), row_map(qk_cb)),
            pl.BlockSpec((L, va), row_map(v_cb)),
            pl.BlockSpec((L, va), row_map(oa_cb)),
            pl.BlockSpec((L, 2 * nh), lambda b, c: (b * nc + c, 0)),
            pl.BlockSpec((1, 2 * nh, L), lambda b, c: (b * nc + c, 0, 0)),
            pl.BlockSpec((cw, qk2), lambda b, c: (0, 0)),
            pl.BlockSpec((1, qk2), lambda b, c: (0, 0)),
            pl.BlockSpec((1, va), lambda b, c: (0, 0)),
            pl.BlockSpec((1, cw - 1, qk2), lambda b, c: (b, 0, 0)),
            pl.BlockSpec((1, nh, dk, dv), lambda b, c: (b, 0, 0, 0)),
            pl.BlockSpec((1, nh, dk), lambda b, c: (b, 0, 0)),
            pl.BlockSpec((1, 1, nh), lambda b, c: (b, 0, 0)),
        ],
        out_specs=[
            pl.BlockSpec((L, va), lambda b, c: (b * nc + c, 0)),
            pl.BlockSpec((1, cw - 1, qk2), lambda b, c: (b, 0, 0)),
            pl.BlockSpec((1, nh, dk, dv), lambda b, c: (b, 0, 0, 0)),
            pl.BlockSpec((1, nh, dk), lambda b, c: (b, 0, 0)),
            pl.BlockSpec((1, 1, nh), lambda b, c: (b, 0, 0)),
        ],
        out_shape=[
            jax.ShapeDtypeStruct((B * T, va), _BF16),
            jax.ShapeDtypeStruct((B, cw - 1, qk2), _F32),
            jax.ShapeDtypeStruct((B, nh, dk, dv), _F32),
            jax.ShapeDtypeStruct((B, nh, dk), _F32),
            jax.ShapeDtypeStruct((B, 1, nh), _F32),
        ],
        scratch_shapes=[pltpu.VMEM((L + 8, qk2), _F32), pltpu.VMEM((L, qk2), _F32)],
        compiler_params=_params("parallel", "arbitrary"),
        name="mlstm_chunks",
    )(proj, proj, proj, gcol, grow, conv_w, conv_b.reshape(1, qk2), gain_a.reshape(1, va),
      conv0, c0, n0, m0.reshape(B, 1, nh))
    y, conv_new, c_new, n_new, m_new = outs
    return y, conv_new, c_new, n_new, m_new.reshape(B, nh)


def _level_reference_rows(b_sc, l, L, width):
    half = 1 << l
    blk = 2 * half
    if blk >= 8:
        pieces = [jnp.broadcast_to(b_sc[i * blk + half - 1:i * blk + half, :], (blk, width))
                  for i in range(L // blk)]
    else:
        sub = lax.broadcasted_iota(jnp.int32, (8, width), 0)
        pieces = []
        for t8 in range(L // 8):
            piece = None
            for i in range(8 // blk):
                r = t8 * 8 + i * blk + half - 1
                cand = jnp.broadcast_to(b_sc[r:r + 1, :], (8, width))
                piece = cand if piece is None else jnp.where(sub >= i * blk, cand, piece)
            pieces.append(piece)
    return jnp.concatenate(pieces, axis=0)


def _hgrn_kernel(hf_ref, hi_ref, hq_ref, hg_ref, lb_ref, gain_ref, tril_ref, s0_ref,
                 y_ref, s_ref, st_sc, e_sc, b_sc, *, hpg, dh, L, nlev):
    c = pl.program_id(2)
    last = pl.num_programs(2) - 1
    width = hpg * dh

    @pl.when(c == 0)
    def _():
        for h in range(hpg):
            st_sc[h] = s0_ref[0, h].T

    lb = lb_ref[...]
    f = lb + (1.0 - lb) * _sigmoid(hf_ref[...])
    lf = jnp.log(f)
    kk = 1.0 - f
    hq = hq_ref[...]
    qq = hq * _sigmoid(hq)
    p0 = lf.astype(_BF16)
    r1 = lf - p0.astype(_F32)
    p1 = r1.astype(_BF16)
    p2 = (r1 - p1.astype(_F32)).astype(_BF16)
    tril = tril_ref[...]
    b = _dot(tril, p0) + _dot(tril, p1) + _dot(tril, p2)
    b_sc[...] = b
    e_sc[0:L, :] = jnp.exp(b)
    e_sc[L:2 * L, :] = jnp.exp(b_sc[L - 1:L, :] - b)
    rowid = lax.broadcasted_iota(jnp.int32, (L, width), 0)
    e_sc[2 * L:3 * L, :] = jnp.where((rowid & 1) == 1, f, 1.0)
    for l in range(1, nlev):
        delta = b - _level_reference_rows(b_sc, l, L, width)
        upper = ((rowid >> l) & 1) == 1
        e_sc[(2 + l) * L:(3 + l) * L, :] = jnp.exp(jnp.where(upper, delta, -delta))

    row = lax.broadcasted_iota(jnp.int32, (L, L), 0)
    col = lax.broadcasted_iota(jnp.int32, (L, L), 1)
    masks = []
    for l in range(nlev):
        same = (row >> (l + 1)) == (col >> (l + 1))
        masks.append(same & (((row >> l) & 1) == 1) & (((col >> l) & 1) == 0))
    diag = row == col

    sls = [slice(h * dh, (h + 1) * dh) for h in range(hpg)]
    ibs = [hi_ref[:, sl].astype(_BF16) for sl in sls]
    amats = []
    for sl in sls:
        q = qq[:, sl]
        k = kk[:, sl]
        amat = jnp.where(diag, jnp.sum(q * k, axis=1, keepdims=True), 0.0)
        for l in range(nlev):
            e = e_sc[(2 + l) * L:(3 + l) * L, sl]
            p = lax.dot_general((q * e).astype(_BF16), (k * e).astype(_BF16), _NT,
                                preferred_element_type=_F32)
            amat = jnp.where(masks[l], p, amat)
        amats.append(amat.astype(_BF16))
    outs, new_states = [], []
    for h, sl in enumerate(sls):
        e_cum = e_sc[0:L, sl]
        e_rev = e_sc[L:2 * L, sl]
        st = st_sc[h]
        outs.append(lax.dot_general((qq[:, sl] * e_cum).astype(_BF16), st.astype(_BF16), _NT,
                                    preferred_element_type=_F32)
                    + _dot(amats[h], ibs[h]))
        new_states.append(st * e_cum[L - 1:L, :]
                          + lax.dot_general(ibs[h], (kk[:, sl] * e_rev).astype(_BF16), _TN,
                                            preferred_element_type=_F32))
    for h, sl in enumerate(sls):
        st_sc[h] = new_states[h]
        o = outs[h]
        yn = o * lax.rsqrt(jnp.mean(o * o, axis=1, keepdims=True) + EPS)
        hg = hg_ref[:, sl]
        y_ref[:, sl] = (yn * gain_ref[:, sl] * (hg * _sigmoid(hg))).astype(y_ref.dtype)

    @pl.when(c == last)
    def _():
        for h in range(hpg):
            s_ref[0, h] = st_sc[h].T


def _hgrn(proj, row0, B, T, s0, lb, gain_b, cols):
    nh, dh = s0.shape[1], s0.shape[2]
    wb = nh * dh
    L = min(T, CHUNK)
    nc = T // L
    rb0 = row0 // L
    hpg = min(HGRN_HEADS_PER_STEP, nh)
    ng = nh // hpg
    gw = hpg * dh
    nlev = int(math.log2(L))
    assert 1 << nlev == L and L % 8 == 0
    tril = jnp.asarray(np.tril(np.ones((L, L), np.float32)), dtype=_BF16)
    nrow = (nlev + 2) * L
    hf_cb, hi_cb, hq_cb, hg_cb = cols
    kern = functools.partial(_hgrn_kernel, hpg=hpg, dh=dh, L=L, nlev=nlev)
    row_map = lambda cb: (lambda b, g, c: (rb0 + b * nc + c, cb * ng + g))
    y, s_new = pl.pallas_call(
        kern,
        grid=(B, ng, nc),
        in_specs=[
            pl.BlockSpec((L, gw), row_map(hf_cb)),
            pl.BlockSpec((L, gw), row_map(hi_cb)),
            pl.BlockSpec((L, gw), row_map(hq_cb)),
            pl.BlockSpec((L, gw), row_map(hg_cb)),
            pl.BlockSpec((1, gw), lambda b, g, c: (0, g)),
            pl.BlockSpec((1, gw), lambda b, g, c: (0, g)),
            pl.BlockSpec((L, L), lambda b, g, c: (0, 0)),
            pl.BlockSpec((1, hpg, dh, dh), lambda b, g, c: (b, g, 0, 0)),
        ],
        out_specs=[
            pl.BlockSpec((L, gw), lambda b, g, c: (b * nc + c, g)),
            pl.BlockSpec((1, hpg, dh, dh), lambda b, g, c: (b, g, 0, 0)),
        ],
        out_shape=[
            jax.ShapeDtypeStruct((B * T, wb), _BF16),
            jax.ShapeDtypeStruct((B, nh, dh, dh), _F32),
        ],
        scratch_shapes=[pltpu.VMEM((hpg, dh, dh), _F32), pltpu.VMEM((nrow, gw), _F32),
                        pltpu.VMEM((L, gw), _F32)],
        compiler_params=_params("parallel", "parallel", "arbitrary"),
        name="hgrn2_chunks",
    )(proj, proj, proj, proj, lb.reshape(1, wb), gain_b.reshape(1, wb), tril, s0)
    return y, s_new


_ROW_DMA_PRIORITY = 1


def _expert_rows_loop(b0, nb, rows, x_hbm, out_hbm, col0, ncol, xbuf, obuf, sem_in, sem_out, compute):
    def x_copy(r, slot):
        return pltpu.make_async_copy(x_hbm.at[pl.ds((b0 + r) * rows, rows), :], xbuf.at[slot], sem_in.at[slot])

    def o_copy(r, slot):
        return pltpu.make_async_copy(obuf.at[slot],
                                     out_hbm.at[pl.ds((b0 + r) * rows, rows), pl.ds(col0, ncol)],
                                     sem_out.at[slot])

    def body(r, carry):
        slot = lax.rem(r, 2)
        x_copy(r, slot).wait()

        @pl.when(r + 1 < nb)
        def _():
            x_copy(r + 1, 1 - slot).start(priority=_ROW_DMA_PRIORITY)

        @pl.when(r >= 2)
        def _():
            o_copy(r - 2, slot).wait()

        obuf[slot] = compute(xbuf[slot])
        o_copy(r, slot).start(priority=_ROW_DMA_PRIORITY)
        return carry

    lax.fori_loop(0, nb, body, 0)

    @pl.when(nb >= 2)
    def _():
        o_copy(nb - 2, lax.rem(nb, 2)).wait()

    o_copy(nb - 1, lax.rem(nb - 1, 2)).wait()


def _start_first_block(cond, expert, bs_ref, nb_ref, rows, x_hbm, xbuf, sem_in):
    e_safe = jnp.minimum(expert, nb_ref.shape[0] - 1)

    @pl.when(cond & (nb_ref[e_safe] > 0))
    def _():
        pltpu.make_async_copy(x_hbm.at[pl.ds(bs_ref[e_safe] * rows, rows), :], xbuf.at[0],
                              sem_in.at[0]).start(priority=_ROW_DMA_PRIORITY)


def _zero_row_blocks(b_from, b_to, rows, out_hbm, col0, ncol, obuf, sem_out):
    obuf[0] = jnp.zeros(obuf.shape[1:], obuf.dtype)

    def body(t, carry):
        cp = pltpu.make_async_copy(obuf.at[0], out_hbm.at[pl.ds(t * rows, rows), pl.ds(col0, ncol)],
                                   sem_out.at[0])
        cp.start()
        cp.wait()
        return carry

    lax.fori_loop(b_from, b_to, body, 0)


def _moe_up_kernel(bs_ref, nb_ref, x_hbm, wg_ref, wu_ref, bg_ref, bu_ref, act_hbm,
                   wg_sc, wu_sc, xbuf, obuf, sem_in, sem_out, *, nj, rows, tn, n_exp, nb_total):
    g = pl.program_id(0)
    e = g // nj
    col0 = pl.multiple_of((g % nj) * tn, tn)
    b0 = bs_ref[e]
    nb = nb_ref[e]

    def compute(x):
        gt = jnp.minimum(_dot(x, wg_sc[...]) + bg_ref[...], SWIGLU_LIMIT)
        up = jnp.clip(_dot(x, wu_sc[...]) + bu_ref[...], -SWIGLU_LIMIT, SWIGLU_LIMIT)
        return ((up + 1.0) * gt * _sigmoid(gt * SWIGLU_ALPHA)).astype(obuf.dtype)

    _start_first_block(g == 0, e, bs_ref, nb_ref, rows, x_hbm, xbuf, sem_in)

    @pl.when(nb > 0)
    def _():
        wg_sc[...] = wg_ref[...].astype(_BF16)
        wu_sc[...] = wu_ref[...].astype(_BF16)
        _expert_rows_loop(b0, nb, rows, x_hbm, act_hbm, col0, tn, xbuf, obuf, sem_in, sem_out, compute)

    @pl.when(e == n_exp - 1)
    def _():
        _zero_row_blocks(b0 + nb, nb_total, rows, act_hbm, col0, tn, obuf, sem_out)

    _start_first_block(g + 1 < pl.num_programs(0), (g + 1) // nj, bs_ref, nb_ref, rows, x_hbm, xbuf, sem_in)


def _moe_down_kernel(bs_ref, nb_ref, x_hbm, wd_ref, bd_ref, y_hbm,
                     wd_sc, xbuf, obuf, sem_in, sem_out, *, nj, rows, tn, n_exp, nb_total):
    g = pl.program_id(0)
    e = g // nj
    col0 = pl.multiple_of((g % nj) * tn, tn)
    b0 = bs_ref[e]
    nb = nb_ref[e]

    def compute(x):
        return _dot(x, wd_sc[...]) + bd_ref[...]

    _start_first_block(g == 0, e, bs_ref, nb_ref, rows, x_hbm, xbuf, sem_in)

    @pl.when(nb > 0)
    def _():
        wd_sc[...] = wd_ref[...].astype(_BF16)
        _expert_rows_loop(b0, nb, rows, x_hbm, y_hbm, col0, tn, xbuf, obuf, sem_in, sem_out, compute)

    @pl.when(e == n_exp - 1)
    def _():
        _zero_row_blocks(b0 + nb, nb_total, rows, y_hbm, col0, tn, obuf, sem_out)

    _start_first_block(g + 1 < pl.num_programs(0), (g + 1) // nj, bs_ref, nb_ref, rows, x_hbm, xbuf, sem_in)


def _moe(xn, logits, w_gate_up, b_gate_up, w_down, b_down):
    nt, d = xn.shape
    e_num, _, de2 = w_gate_up.shape
    de = de2 // 2
    rows = MOE_ROWS
    na = nt * TOP_K
    top_v, top_e = lax.top_k(logits, TOP_K)
    gate = jax.nn.softmax(top_v, axis=-1)
    flat_e = top_e.reshape(-1).astype(jnp.int32)
    order = jnp.argsort(flat_e).astype(jnp.int32)
    rank = jnp.argsort(order).astype(jnp.int32)
    counts = jnp.sum(flat_e[:, None] == jnp.arange(e_num, dtype=jnp.int32)[None, :], axis=0).astype(jnp.int32)
    nblk = (counts + rows - 1) // rows
    blk_end = jnp.cumsum(nblk)
    blk_start = blk_end - nblk
    sort_start = jnp.cumsum(counts) - counts
    nb_total = -(-na // rows) + e_num
    n_slots = nb_total * rows
    slot_of = blk_start[flat_e] * rows + rank - sort_start[flat_e]
    s = jnp.arange(n_slots, dtype=jnp.int32)
    e_s = jnp.minimum(jnp.sum((s // rows)[:, None] >= blk_end[None, :], axis=1), e_num - 1).astype(jnp.int32)
    pos = s - blk_start[e_s] * rows
    src = jnp.clip(sort_start[e_s] + pos, 0, na - 1)
    slot_tok = jnp.where(pos < counts[e_s], order[src] // TOP_K, s % nt)
    x_sorted = jnp.take(xn, slot_tok, axis=0, mode="clip")

    tn = _tile(de, 512, _LANES)
    nj = de // tn
    bgu = b_gate_up.reshape(e_num, 1, de2)
    hbm = pl.BlockSpec(memory_space=pl.ANY)
    dma2 = pltpu.SemaphoreType.DMA((2,))
    act = pl.pallas_call(
        functools.partial(_moe_up_kernel, nj=nj, rows=rows, tn=tn, n_exp=e_num, nb_total=nb_total),
        grid_spec=pltpu.PrefetchScalarGridSpec(
            num_scalar_prefetch=2,
            grid=(e_num * nj,),
            in_specs=[
                hbm,
                pl.BlockSpec((None, d, tn), lambda g, bs, nb: (g // nj, 0, g % nj)),
                pl.BlockSpec((None, d, tn), lambda g, bs, nb: (g // nj, 0, nj + g % nj)),
                pl.BlockSpec((None, 1, tn), lambda g, bs, nb: (g // nj, 0, g % nj)),
                pl.BlockSpec((None, 1, tn), lambda g, bs, nb: (g // nj, 0, nj + g % nj)),
            ],
            out_specs=hbm,
            scratch_shapes=[pltpu.VMEM((d, tn), _BF16), pltpu.VMEM((d, tn), _BF16),
                            pltpu.VMEM((2, rows, d), _BF16), pltpu.VMEM((2, rows, tn), _BF16), dma2, dma2],
        ),
        out_shape=jax.ShapeDtypeStruct((n_slots, de), _BF16),
        compiler_params=_params("arbitrary"),
        name="moe_gate_up",
    )(blk_start, nblk, x_sorted, w_gate_up, w_gate_up, bgu, bgu)

    tn2 = _tile(d, 1024, _LANES)
    nj2 = d // tn2
    y_sorted = pl.pallas_call(
        functools.partial(_moe_down_kernel, nj=nj2, rows=rows, tn=tn2, n_exp=e_num, nb_total=nb_total),
        grid_spec=pltpu.PrefetchScalarGridSpec(
            num_scalar_prefetch=2,
            grid=(e_num * nj2,),
            in_specs=[
                hbm,
                pl.BlockSpec((None, de, tn2), lambda g, bs, nb: (g // nj2, 0, g % nj2)),
                pl.BlockSpec((None, 1, tn2), lambda g, bs, nb: (g // nj2, 0, g % nj2)),
            ],
            out_specs=hbm,
            scratch_shapes=[pltpu.VMEM((de, tn2), _BF16),
                            pltpu.VMEM((2, rows, de), _BF16), pltpu.VMEM((2, rows, tn2), _F32), dma2, dma2],
        ),
        out_shape=jax.ShapeDtypeStruct((n_slots, d), _F32),
        compiler_params=_params("arbitrary"),
        name="moe_down",
    )(blk_start, nblk, act, w_down, b_down.reshape(e_num, 1, d))

    idx = slot_of.reshape(nt, TOP_K).T.reshape(-1)
    yk = jnp.take(y_sorted, idx, axis=0, mode="clip").reshape(TOP_K, nt, d)
    return yk, gate


def kernel(x_prompt, x_sample, state_conv, state_mlstm_C, state_mlstm_n, state_mlstm_m, state_hgrn_S,
           g_mix, w_in, b_in, conv_w, conv_b, gain_a, gain_b, lb_logits, w_branch_a, w_branch_b, w_out,
           g_ffn, w_router, b_router, w_gate_up, b_gate_up, w_down, b_down, g_final):
    depth = w_in.shape[0]
    assert depth == 1, "single-layer step"
    bp, tp, d = x_prompt.shape
    bs, ts, _ = x_sample.shape
    nh_a, dk_a, dv_a = state_mlstm_C.shape[2:]
    nh_b, dh_b = state_hgrn_S.shape[2:4]
    qk2, va, wb = 2 * nh_a * dk_a, nh_a * dv_a, nh_b * dh_b
    assert qk2 == d and va == d and wb == d, "column blocks of the projection must have equal width"
    np_rows, ns_rows = bp * tp, bs * ts
    l = 0

    x = jnp.concatenate([x_prompt.reshape(np_rows, d), x_sample.reshape(ns_rows, d)], axis=0)

    sizes = (qk2, va, va, nh_a, nh_a, wb, wb, wb, wb, d, d)
    offs = np.concatenate([[0], np.cumsum(sizes)])
    big = [i for i, s in enumerate(sizes) if s == d]
    wt_l, b_l = jnp.swapaxes(w_in[l], 0, 1), b_in[l]
    wt_main = _pack_rows(wt_l, int(offs[3]), int(offs[5]), min(d, 512))
    b_main = jnp.concatenate([b_l[offs[i]:offs[i + 1]] for i in big])
    ng = 2 * nh_a
    wt_gate = jnp.pad(wt_l[offs[3]:offs[5], :], ((0, _LANES - ng), (0, 0))).astype(_BF16)
    b_gate = jnp.pad(b_l[offs[3]:offs[5]], (0, _LANES - ng))

    u = _rms_cast(x, g_mix[l], _BF16)
    proj = _mm_bias_nt(u, wt_main, b_main, _F32, tm_pref=832)
    gates = _mm_bias_nt(u, wt_gate, b_gate, _F32, tn_pref=_LANES)[:, :ng]

    lb = jnp.cumsum(jax.nn.softmax(lb_logits.astype(_F32), axis=0), axis=0)[l]

    def mixers(row0, B, T, conv0, c0, n0, m0, s0):
        ya, cv, cn, nn, mn = _mlstm(proj, gates, row0, B, T, conv0, c0, n0, m0,
                                    conv_w[l], conv_b[l], gain_a[l], cols=(0, 1, 2))
        yb, sn = _hgrn(proj, row0, B, T, s0, lb, gain_b[l], cols=(3, 4, 5, 6))
        return ya, yb, (cv, cn, nn, mn, sn)

    zeros = lambda *s: jnp.zeros(s, _F32)
    ya_p, yb_p, st_p = mixers(0, bp, tp, zeros(bp, conv_w.shape[1] - 1, qk2), zeros(bp, nh_a, dk_a, dv_a),
                              zeros(bp, nh_a, dk_a), zeros(bp, nh_a), zeros(bp, nh_b, dh_b, dh_b))
    ya_s, yb_s, st_s = mixers(np_rows, bs, ts, state_conv[l], state_mlstm_C[l], state_mlstm_n[l],
                              state_mlstm_m[l], state_hgrn_S[l])
    ya = jnp.concatenate([ya_p, ya_s], axis=0)
    yb = jnp.concatenate([yb_p, yb_s], axis=0)

    z = _merge(ya, yb, w_branch_a[l].astype(_BF16), w_branch_b[l].astype(_BF16), proj, 7, 8)
    h = _outproj(z, w_out[l].astype(_BF16), x)

    xn, logits = _norm_router(h, g_ffn[l], w_router[l], b_router[l])
    yk, gate = _moe(xn, logits, w_gate_up[l], b_gate_up[l], w_down[l], b_down[l])
    out_p, out_s = _final_norm(h, yk, gate, g_final, np_rows, ns_rows)

    states = [s[None] for s in st_p] + [s[None] for s in st_s]
    return (out_p.reshape(bp, tp, d), out_s.reshape(bs, ts, d), *states)
```

```python
import functools
import math

import numpy as np
import jax
import jax.numpy as jnp
from jax import lax
from jax.experimental import pallas as pl
from jax.experimental.pallas import tpu as pltpu

_F32 = jnp.float32
_BF16 = jnp.bfloat16

EPS = 1e-6
TOP_K = 4
CHUNK = 64
SWIGLU_LIMIT = 7.0
SWIGLU_ALPHA = 1.702
MOE_ROWS = 512
HGRN_HEADS_PER_STEP = 16

_VMEM_LIMIT_BYTES = 56 * 1024 * 1024
_LANES = 128

_NT = (((1,), (1,)), ((), ()))
_TN = (((0,), (0,)), ((), ()))


def _params(*sem):
    return pltpu.CompilerParams(dimension_semantics=sem, vmem_limit_bytes=_VMEM_LIMIT_BYTES)


def _tile(n, pref, mult):
    best = None
    for t in range(mult, min(n, pref) + 1, mult):
        if n % t == 0:
            best = t
    return best if best is not None else n


def _sigmoid(x):
    return 1.0 / (1.0 + jnp.exp(-x))


def _dot(a, b):
    return jnp.dot(a, b, preferred_element_type=_F32)


def _rms_cast_kernel(x_ref, g_ref, o_ref):
    x = x_ref[...]
    y = x * lax.rsqrt(jnp.mean(x * x, axis=-1, keepdims=True) + EPS) * g_ref[...]
    o_ref[...] = y.astype(o_ref.dtype)


def _rms_cast(x, g, out_dtype):
    n, d = x.shape
    tm = _tile(n, 512, 8)
    return pl.pallas_call(
        _rms_cast_kernel,
        grid=(n // tm,),
        in_specs=[pl.BlockSpec((tm, d), lambda i: (i, 0)), pl.BlockSpec((1, d), lambda i: (0, 0))],
        out_specs=pl.BlockSpec((tm, d), lambda i: (i, 0)),
        out_shape=jax.ShapeDtypeStruct((n, d), out_dtype),
        compiler_params=_params("parallel"),
        name="rms_cast",
    )(x, g.reshape(1, d))


def _pack_rows_kernel(a_ref, b_ref, o_ref, *, shift, first_shifted):
    i = pl.program_id(0)

    @pl.when(i < first_shifted)
    def _():
        o_ref[...] = a_ref[...].astype(o_ref.dtype)

    @pl.when(i >= first_shifted)
    def _():
        tb = o_ref.shape[0]
        o_ref[0:tb - shift, :] = a_ref[shift:tb, :].astype(o_ref.dtype)
        o_ref[tb - shift:tb, :] = b_ref[...].astype(o_ref.dtype)


def _pack_rows(wt, cut0, cut1, tb):
    n, k = wt.shape
    shift = cut1 - cut0
    n_out = n - shift
    assert cut0 % tb == 0 and n_out % tb == 0 and shift % 16 == 0 and tb % shift == 0
    return pl.pallas_call(
        functools.partial(_pack_rows_kernel, shift=shift, first_shifted=cut0 // tb),
        grid=(n_out // tb,),
        in_specs=[
            pl.BlockSpec((tb, k), lambda i: (i, 0)),
            pl.BlockSpec((shift, k), lambda i: ((i + 1) * (tb // shift), 0)),
        ],
        out_specs=pl.BlockSpec((tb, k), lambda i: (i, 0)),
        out_shape=jax.ShapeDtypeStruct((n_out, k), _BF16),
        compiler_params=_params("parallel"),
        name="pack_w_in",
    )(wt, wt)


def _mm_nt_kernel(x_ref, wt_ref, b_ref, o_ref):
    acc = lax.dot_general(x_ref[...], wt_ref[...], _NT, preferred_element_type=_F32)
    o_ref[...] = (acc + b_ref[...]).astype(o_ref.dtype)


def _mm_bias_nt(x, wt, b, out_dtype, tm_pref=640, tn_pref=1024):
    m, k = x.shape
    n = wt.shape[0]
    tm = _tile(m, tm_pref, 8)
    tn = _tile(n, tn_pref, _LANES)
    return pl.pallas_call(
        _mm_nt_kernel,
        grid=(m // tm, n // tn),
        in_specs=[
            pl.BlockSpec((tm, k), lambda i, j: (i, 0)),
            pl.BlockSpec((tn, k), lambda i, j: (j, 0)),
            pl.BlockSpec((1, tn), lambda i, j: (0, j)),
        ],
        out_specs=pl.BlockSpec((tm, tn), lambda i, j: (i, j)),
        out_shape=jax.ShapeDtypeStruct((m, n), out_dtype),
        compiler_params=_params("parallel", "parallel"),
        name="mm_bias",
    )(x, wt, b.reshape(1, n))


def _merge_kernel(ya_ref, yb_ref, wa_ref, wb_ref, ga_ref, gb_ref, o_ref):
    a = _dot(ya_ref[...], wa_ref[...])
    b = _dot(yb_ref[...], wb_ref[...])
    z = _sigmoid(ga_ref[...]) * a + _sigmoid(gb_ref[...]) * b
    o_ref[...] = z.astype(o_ref.dtype)


def _merge(ya, yb, wa, wb, proj, ga_col, gb_col):
    m, k = ya.shape
    n = wa.shape[1]
    tm = _tile(m, 640, 8)
    tn = _tile(n, 512, _LANES)
    nj = n // tn
    return pl.pallas_call(
        _merge_kernel,
        grid=(m // tm, nj),
        in_specs=[
            pl.BlockSpec((tm, k), lambda i, j: (i, 0)),
            pl.BlockSpec((tm, k), lambda i, j: (i, 0)),
            pl.BlockSpec((k, tn), lambda i, j: (0, j)),
            pl.BlockSpec((k, tn), lambda i, j: (0, j)),
            pl.BlockSpec((tm, tn), lambda i, j: (i, ga_col * nj + j)),
            pl.BlockSpec((tm, tn), lambda i, j: (i, gb_col * nj + j)),
        ],
        out_specs=pl.BlockSpec((tm, tn), lambda i, j: (i, j)),
        out_shape=jax.ShapeDtypeStruct((m, n), _BF16),
        compiler_params=_params("parallel", "parallel"),
        name="branch_merge",
    )(ya, yb, wa, wb, proj, proj)


def _outproj_kernel(z_ref, w_ref, x_ref, o_ref):
    o_ref[...] = x_ref[...] + _dot(z_ref[...], w_ref[...])


def _outproj(z, w, x):
    m, k = z.shape
    n = w.shape[1]
    tm = _tile(m, 640, 8)
    tn = _tile(n, 1024, _LANES)
    return pl.pallas_call(
        _outproj_kernel,
        grid=(m // tm, n // tn),
        in_specs=[
            pl.BlockSpec((tm, k), lambda i, j: (i, 0)),
            pl.BlockSpec((k, tn), lambda i, j: (0, j)),
            pl.BlockSpec((tm, tn), lambda i, j: (i, j)),
        ],
        out_specs=pl.BlockSpec((tm, tn), lambda i, j: (i, j)),
        out_shape=jax.ShapeDtypeStruct((m, n), _F32),
        compiler_params=_params("parallel", "parallel"),
        name="out_proj_residual",
    )(z, w, x)


def _split_bf16(x):
    hi = x.astype(_BF16)
    lo = (x - hi.astype(_F32)).astype(_BF16)
    return hi, lo


def _norm_router_kernel(h_ref, g_ref, wr_ref, br_ref, xn_ref, lg_ref):
    h = h_ref[...]
    xn = h * lax.rsqrt(jnp.mean(h * h, axis=-1, keepdims=True) + EPS) * g_ref[...]
    xn_ref[...] = xn.astype(xn_ref.dtype)
    xh, xl = _split_bf16(xn)
    wh, wl = _split_bf16(wr_ref[...])
    lg_ref[...] = _dot(xh, wh) + _dot(xl, wh) + _dot(xh, wl) + br_ref[...]


def _norm_router(h, g, w_router, b_router):
    n, d = h.shape
    e = w_router.shape[1]
    ep = -(-e // _LANES) * _LANES
    wr = jnp.pad(w_router.astype(_F32), ((0, 0), (0, ep - e)))
    br = jnp.pad(b_router.astype(_F32), (0, ep - e)).reshape(1, ep)
    tm = _tile(n, 512, 8)
    xn, lg = pl.pallas_call(
        _norm_router_kernel,
        grid=(n // tm,),
        in_specs=[
            pl.BlockSpec((tm, d), lambda i: (i, 0)),
            pl.BlockSpec((1, d), lambda i: (0, 0)),
            pl.BlockSpec((d, ep), lambda i: (0, 0)),
            pl.BlockSpec((1, ep), lambda i: (0, 0)),
        ],
        out_specs=[pl.BlockSpec((tm, d), lambda i: (i, 0)), pl.BlockSpec((tm, ep), lambda i: (i, 0))],
        out_shape=[jax.ShapeDtypeStruct((n, d), _BF16), jax.ShapeDtypeStruct((n, ep), _F32)],
        compiler_params=_params("parallel"),
        name="ffn_norm_router",
    )(h, g.reshape(1, d), wr, br)
    return xn, lg[:, :e]


def _final_kernel(h_ref, yk_ref, gate_ref, g_ref, op_ref, os_ref, *, n_prompt_tiles):
    i = pl.program_id(0)
    h = h_ref[...]
    for k in range(TOP_K):
        h = h + gate_ref[:, k:k + 1] * yk_ref[k]
    out = h * lax.rsqrt(jnp.mean(h * h, axis=-1, keepdims=True) + EPS) * g_ref[...]

    @pl.when(i < n_prompt_tiles)
    def _():
        op_ref[...] = out

    @pl.when(i >= n_prompt_tiles)
    def _():
        os_ref[...] = out


def _final_norm(h, yk, gate, g, n_prompt, n_sample):
    n, d = h.shape
    tm = _tile(math.gcd(n_prompt, n_sample), 128, 8)
    npt = n_prompt // tm
    return pl.pallas_call(
        functools.partial(_final_kernel, n_prompt_tiles=npt),
        grid=(n // tm,),
        in_specs=[
            pl.BlockSpec((tm, d), lambda i: (i, 0)),
            pl.BlockSpec((TOP_K, tm, d), lambda i: (0, i, 0)),
            pl.BlockSpec((tm, TOP_K), lambda i: (i, 0)),
            pl.BlockSpec((1, d), lambda i: (0, 0)),
        ],
        out_specs=[
            pl.BlockSpec((tm, d), lambda i: (jnp.minimum(i, npt - 1), 0)),
            pl.BlockSpec((tm, d), lambda i: (jnp.maximum(i - npt, 0), 0)),
        ],
        out_shape=[jax.ShapeDtypeStruct((n_prompt, d), _F32), jax.ShapeDtypeStruct((n_sample, d), _F32)],
        compiler_params=_params("arbitrary"),
        name="moe_combine_final_norm",
    )(h, yk, gate, g.reshape(1, d))


def _mlstm_kernel(qk_ref, v_ref, oa_ref, gcol_ref, grow_ref, cw_ref, cb_ref, gain_ref,
                  conv0_ref, c0_ref, n0_ref, m0_ref,
                  y_ref, conv_ref, c_ref, n_ref, m_ref,
                  xp_sc, qk_sc, *, nh, dk, dv, L, cw):
    c = pl.program_id(1)
    last = pl.num_programs(1) - 1
    pad = 8
    hist = cw - 1

    @pl.when(c == 0)
    def _():
        c_ref[...] = c0_ref[...]
        n_ref[...] = n0_ref[...]
        m_ref[...] = m0_ref[...]
        xp_sc[pad - hist:pad, :] = conv0_ref[0]

    @pl.when(c > 0)
    def _():
        xp_sc[pad - hist:pad, :] = xp_sc[pad + L - hist:pad + L, :]

    xp_sc[pad:pad + L, :] = qk_ref[...]

    @pl.when(c == last)
    def _():
        conv_ref[0] = xp_sc[pad + L - hist:pad + L, :]

    acc = cb_ref[...] + xp_sc[pad - hist:pad - hist + L, :] * cw_ref[0:1, :]
    for j in range(1, cw):
        acc = acc + xp_sc[pad - hist + j:pad - hist + j + L, :] * cw_ref[j:j + 1, :]
    qk_sc[...] = acc * _sigmoid(acc)

    row = lax.broadcasted_iota(jnp.int32, (L, L), 0)
    col = lax.broadcasted_iota(jnp.int32, (L, L), 1)
    causal = col <= row
    qk_off = nh * dk

    pending = []
    for h in range(nh):
        q = qk_sc[:, h * dk:(h + 1) * dk]
        k = qk_sc[:, qk_off + h * dk:qk_off + (h + 1) * dk] * (dk ** -0.5)
        v = v_ref[:, h * dv:(h + 1) * dv]
        f_col = gcol_ref[:, h:h + 1]
        i_col = gcol_ref[:, nh + h:nh + h + 1]
        f_row = grow_ref[0, h:h + 1, :]
        i_row = grow_ref[0, nh + h:nh + h + 1, :]
        m_prev = m_ref[0, :, h:h + 1]
        cmat = c_ref[0, h]
        nvec = n_ref[0, h:h + 1, :]

        d = jnp.where(causal, f_col - f_row + i_row, -jnp.inf)
        g = f_col + m_prev
        m_t = jnp.maximum(g, jnp.max(d, axis=1, keepdims=True))
        w = jnp.exp(d - m_t)
        a = jnp.exp(g - m_t)
        qb = q.astype(_BF16)
        vb = v.astype(_BF16)
        s = lax.dot_general(qb, k.astype(_BF16), _NT, preferred_element_type=_F32) * w
        num = a * _dot(qb, cmat.astype(_BF16)) + _dot(s.astype(_BF16), vb)
        den = a * jnp.sum(q * nvec, axis=1, keepdims=True) + jnp.sum(s, axis=1, keepdims=True)
        hh = num / jnp.maximum(jnp.abs(den), jnp.exp(-m_t))

        m_last = m_t[L - 1:L, :]
        a_last = a[L - 1:L, :]
        w_last = jnp.exp(f_col[L - 1:L, :] - f_col + i_col - m_last)
        kw = k * w_last
        yn = hh * lax.rsqrt(jnp.mean(hh * hh, axis=1, keepdims=True) + EPS)
        sl = slice(h * dv, (h + 1) * dv)
        pending.append((
            (yn * gain_ref[:, sl] * _sigmoid(oa_ref[:, sl])).astype(y_ref.dtype),
            lax.dot_general(kw.astype(_BF16), vb, _TN, preferred_element_type=_F32),
            a_last, a_last * nvec + jnp.sum(kw, axis=0, keepdims=True), m_last))

    for h, (y_h, kv_h, a_last, n_new, m_last) in enumerate(pending):
        y_ref[:, h * dv:(h + 1) * dv] = y_h
        c_ref[0, h] = a_last * c_ref[0, h] + kv_h
        n_ref[0, h:h + 1, :] = n_new
        m_ref[0, :, h:h + 1] = m_last


def _mlstm(proj, gates, row0, B, T, conv0, c0, n0, m0, conv_w, conv_b, gain_a, cols):
    nh, dk, dv = c0.shape[1], c0.shape[2], c0.shape[3]
    cw = conv_w.shape[0]
    L = min(T, CHUNK)
    nc = T // L
    qk2, va = 2 * nh * dk, nh * dv
    rb0 = row0 // L
    gt = lax.dynamic_slice_in_dim(gates, row0, B * T, axis=0)
    ig = gt[:, :nh].reshape(B, nc, L, nh)
    fcum = jnp.cumsum(jax.nn.log_sigmoid(gt[:, nh:2 * nh]).reshape(B, nc, L, nh), axis=2)
    both = jnp.concatenate([fcum, ig], axis=-1)
    gcol = both.reshape(B * T, 2 * nh)
    grow = jnp.transpose(both, (0, 1, 3, 2)).reshape(B * nc, 2 * nh, L)

    qk_cb, v_cb, oa_cb = cols
    kern = functools.partial(_mlstm_kernel, nh=nh, dk=dk, dv=dv, L=L, cw=cw)
    row_map = lambda cb: (lambda b, c: (rb0 + b * nc + c, cb))
    outs = pl.pallas_call(
        kern,
        grid=(B, nc),
        in_specs=[
            pl.BlockSpec((L, qk2), row_map(qk_cb)),
            pl.BlockSpec((L, va), row_map(v_cb)),
            pl.BlockSpec((L, va), row_map(oa_cb)),
            pl.BlockSpec((L, 2 * nh), lambda b, c: (b * nc + c, 0)),
            pl.BlockSpec((1, 2 * nh, L), lambda b, c: (b * nc + c, 0, 0)),
            pl.BlockSpec((cw, qk2), lambda b, c: (0, 0)),
            pl.BlockSpec((1, qk2), lambda b, c: (0, 0)),
            pl.BlockSpec((1, va), lambda b, c: (0, 0)),
            pl.BlockSpec((1, cw - 1, qk2), lambda b, c: (b, 0, 0)),
            pl.BlockSpec((1, nh, dk, dv), lambda b, c: (b, 0, 0, 0)),
            pl.BlockSpec((1, nh, dk), lambda b, c: (b, 0, 0)),
            pl.BlockSpec((1, 1, nh), lambda b, c: (b, 0, 0)),
        ],
        out_specs=[
            pl.BlockSpec((L, va), lambda b, c: (b * nc + c, 0)),
            pl.BlockSpec((1, cw - 1, qk2), lambda b, c: (b, 0, 0)),
            pl.BlockSpec((1, nh, dk, dv), lambda b, c: (b, 0, 0, 0)),
            pl.BlockSpec((1, nh, dk), lambda b, c: (b, 0, 0)),
            pl.BlockSpec((1, 1, nh), lambda b, c: (b, 0, 0)),
        ],
        out_shape=[
            jax.ShapeDtypeStruct((B * T, va), _BF16),
            jax.ShapeDtypeStruct((B, cw - 1, qk2), _F32),
            jax.ShapeDtypeStruct((B, nh, dk, dv), _F32),
            jax.ShapeDtypeStruct((B, nh, dk), _F32),
            jax.ShapeDtypeStruct((B, 1, nh), _F32),
        ],
        scratch_shapes=[pltpu.VMEM((L + 8, qk2), _F32), pltpu.VMEM((L, qk2), _F32)],
        compiler_params=_params("parallel", "arbitrary"),
        name="mlstm_chunks",
    )(proj, proj, proj, gcol, grow, conv_w, conv_b.reshape(1, qk2), gain_a.reshape(1, va),
      conv0, c0, n0, m0.reshape(B, 1, nh))
    y, conv_new, c_new, n_new, m_new = outs
    return y, conv_new, c_new, n_new, m_new.reshape(B, nh)


def _level_reference_rows(b_sc, l, L, width):
    half = 1 << l
    blk = 2 * half
    if blk >= 8:
        pieces = [jnp.broadcast_to(b_sc[i * blk + half - 1:i * blk + half, :], (blk, width))
                  for i in range(L // blk)]
    else:
        sub = lax.broadcasted_iota(jnp.int32, (8, width), 0)
        pieces = []
        for t8 in range(L // 8):
            piece = None
            for i in range(8 // blk):
                r = t8 * 8 + i * blk + half - 1
                cand = jnp.broadcast_to(b_sc[r:r + 1, :], (8, width))
                piece = cand if piece is None else jnp.where(sub >= i * blk, cand, piece)
            pieces.append(piece)
    return jnp.concatenate(pieces, axis=0)


def _hgrn_kernel(hf_ref, hi_ref, hq_ref, hg_ref, lb_ref, gain_ref, tril_ref, s0_ref,
                 y_ref, s_ref, st_sc, e_sc, b_sc, *, hpg, dh, L, nlev):
    c = pl.program_id(2)
    last = pl.num_programs(2) - 1
    width = hpg * dh

    @pl.when(c == 0)
    def _():
        for h in range(hpg):
            st_sc[h] = s0_ref[0, h].T

    lb = lb_ref[...]
    f = lb + (1.0 - lb) * _sigmoid(hf_ref[...])
    lf = jnp.log(f)
    kk = 1.0 - f
    hq = hq_ref[...]
    qq = hq * _sigmoid(hq)
    p0 = lf.astype(_BF16)
    r1 = lf - p0.astype(_F32)
    p1 = r1.astype(_BF16)
    p2 = (r1 - p1.astype(_F32)).astype(_BF16)
    tril = tril_ref[...]
    b = _dot(tril, p0) + _dot(tril, p1) + _dot(tril, p2)
    b_sc[...] = b
    e_sc[0:L, :] = jnp.exp(b)
    e_sc[L:2 * L, :] = jnp.exp(b_sc[L - 1:L, :] - b)
    rowid = lax.broadcasted_iota(jnp.int32, (L, width), 0)
    e_sc[2 * L:3 * L, :] = jnp.where((rowid & 1) == 1, f, 1.0)
    for l in range(1, nlev):
        delta = b - _level_reference_rows(b_sc, l, L, width)
        upper = ((rowid >> l) & 1) == 1
        e_sc[(2 + l) * L:(3 + l) * L, :] = jnp.exp(jnp.where(upper, delta, -delta))

    row = lax.broadcasted_iota(jnp.int32, (L, L), 0)
    col = lax.broadcasted_iota(jnp.int32, (L, L), 1)
    masks = []
    for l in range(nlev):
        same = (row >> (l + 1)) == (col >> (l + 1))
        masks.append(same & (((row >> l) & 1) == 1) & (((col >> l) & 1) == 0))
    diag = row == col

    sls = [slice(h * dh, (h + 1) * dh) for h in range(hpg)]
    ibs = [hi_ref[:, sl].astype(_BF16) for sl in sls]
    amats = []
    for sl in sls:
        q = qq[:, sl]
        k = kk[:, sl]
        amat = jnp.where(diag, jnp.sum(q * k, axis=1, keepdims=True), 0.0)
        for l in range(nlev):
            e = e_sc[(2 + l) * L:(3 + l) * L, sl]
            p = lax.dot_general((q * e).astype(_BF16), (k * e).astype(_BF16), _NT,
                                preferred_element_type=_F32)
            amat = jnp.where(masks[l], p, amat)
        amats.append(amat.astype(_BF16))
    outs, new_states = [], []
    for h, sl in enumerate(sls):
        e_cum = e_sc[0:L, sl]
        e_rev = e_sc[L:2 * L, sl]
        st = st_sc[h]
        outs.append(lax.dot_general((qq[:, sl] * e_cum).astype(_BF16), st.astype(_BF16), _NT,
                                    preferred_element_type=_F32)
                    + _dot(amats[h], ibs[h]))
        new_states.append(st * e_cum[L - 1:L, :]
                          + lax.dot_general(ibs[h], (kk[:, sl] * e_rev).astype(_BF16), _TN,
                                            preferred_element_type=_F32))
    for h, sl in enumerate(sls):
        st_sc[h] = new_states[h]
        o = outs[h]
        yn = o * lax.rsqrt(jnp.mean(o * o, axis=1, keepdims=True) + EPS)
        hg = hg_ref[:, sl]
        y_ref[:, sl] = (yn * gain_ref[:, sl] * (hg * _sigmoid(hg))).astype(y_ref.dtype)

    @pl.when(c == last)
    def _():
        for h in range(hpg):
            s_ref[0, h] = st_sc[h].T


def _hgrn(proj, row0, B, T, s0, lb, gain_b, cols):
    nh, dh = s0.shape[1], s0.shape[2]
    wb = nh * dh
    L = min(T, CHUNK)
    nc = T // L
    rb0 = row0 // L
    hpg = min(HGRN_HEADS_PER_STEP, nh)
    ng = nh // hpg
    gw = hpg * dh
    nlev = int(math.log2(L))
    assert 1 << nlev == L and L % 8 == 0
    tril = jnp.asarray(np.tril(np.ones((L, L), np.float32)), dtype=_BF16)
    nrow = (nlev + 2) * L
    hf_cb, hi_cb, hq_cb, hg_cb = cols
    kern = functools.partial(_hgrn_kernel, hpg=hpg, dh=dh, L=L, nlev=nlev)
    row_map = lambda cb: (lambda b, g, c: (rb0 + b * nc + c, cb * ng + g))
    y, s_new = pl.pallas_call(
        kern,
        grid=(B, ng, nc),
        in_specs=[
            pl.BlockSpec((L, gw), row_map(hf_cb)),
            pl.BlockSpec((L, gw), row_map(hi_cb)),
            pl.BlockSpec((L, gw), row_map(hq_cb)),
            pl.BlockSpec((L, gw), row_map(hg_cb)),
            pl.BlockSpec((1, gw), lambda b, g, c: (0, g)),
            pl.BlockSpec((1, gw), lambda b, g, c: (0, g)),
            pl.BlockSpec((L, L), lambda b, g, c: (0, 0)),
            pl.BlockSpec((1, hpg, dh, dh), lambda b, g, c: (b, g, 0, 0)),
        ],
        out_specs=[
            pl.BlockSpec((L, gw), lambda b, g, c: (b * nc + c, g)),
            pl.BlockSpec((1, hpg, dh, dh), lambda b, g, c: (b, g, 0, 0)),
        ],
        out_shape=[
            jax.ShapeDtypeStruct((B * T, wb), _BF16),
            jax.ShapeDtypeStruct((B, nh, dh, dh), _F32),
        ],
        scratch_shapes=[pltpu.VMEM((hpg, dh, dh), _F32), pltpu.VMEM((nrow, gw), _F32),
                        pltpu.VMEM((L, gw), _F32)],
        compiler_params=_params("parallel", "parallel", "arbitrary"),
        name="hgrn2_chunks",
    )(proj, proj, proj, proj, lb.reshape(1, wb), gain_b.reshape(1, wb), tril, s0)
    return y, s_new


_FLAG_COMPUTE, _FLAG_CAST, _FLAG_HALF = 1, 2, 4


def _swiglu_block(x, wg, wu, bg, bu, out_dtype):
    gt = jnp.minimum(_dot(x, wg) + bg, SWIGLU_LIMIT)
    up = jnp.clip(_dot(x, wu) + bu, -SWIGLU_LIMIT, SWIGLU_LIMIT)
    return ((up + 1.0) * gt * _sigmoid(gt * SWIGLU_ALPHA)).astype(out_dtype)


def _moe_up_kernel(blk_ref, we_ref, wj_ref, oj_ref, fl_ref,
                   x_ref, wg_ref, wu_ref, bg_ref, bu_ref, o_ref, wg_sc, wu_sc):
    w = pl.program_id(0)
    fl = fl_ref[w]
    half = o_ref.shape[0] // 2

    @pl.when((fl & _FLAG_CAST) != 0)
    def _():
        wg_sc[...] = wg_ref[...].astype(_BF16)
        wu_sc[...] = wu_ref[...].astype(_BF16)

    @pl.when((fl & (_FLAG_COMPUTE | _FLAG_HALF)) == _FLAG_COMPUTE)
    def _():
        o_ref[...] = _swiglu_block(x_ref[...], wg_sc[...], wu_sc[...], bg_ref[...], bu_ref[...], o_ref.dtype)

    @pl.when((fl & (_FLAG_COMPUTE | _FLAG_HALF)) == (_FLAG_COMPUTE | _FLAG_HALF))
    def _():
        o_ref[0:half, :] = _swiglu_block(x_ref[0:half, :], wg_sc[...], wu_sc[...], bg_ref[...], bu_ref[...],
                                         o_ref.dtype)
        o_ref[half:, :] = jnp.zeros((o_ref.shape[0] - half, o_ref.shape[1]), o_ref.dtype)

    @pl.when((fl & _FLAG_COMPUTE) == 0)
    def _():
        o_ref[...] = jnp.zeros_like(o_ref)


def _moe_down_kernel(blk_ref, we_ref, wj_ref, oj_ref, fl_ref,
                     x_ref, wd_ref, bd_ref, o_ref, wd_sc):
    w = pl.program_id(0)
    fl = fl_ref[w]
    half = o_ref.shape[0] // 2

    @pl.when((fl & _FLAG_CAST) != 0)
    def _():
        wd_sc[...] = wd_ref[...].astype(_BF16)

    @pl.when((fl & (_FLAG_COMPUTE | _FLAG_HALF)) == _FLAG_COMPUTE)
    def _():
        o_ref[...] = _dot(x_ref[...], wd_sc[...]) + bd_ref[...]

    @pl.when((fl & (_FLAG_COMPUTE | _FLAG_HALF)) == (_FLAG_COMPUTE | _FLAG_HALF))
    def _():
        o_ref[0:half, :] = _dot(x_ref[0:half, :], wd_sc[...]) + bd_ref[...]
        o_ref[half:, :] = jnp.zeros((o_ref.shape[0] - half, o_ref.shape[1]), o_ref.dtype)

    @pl.when((fl & _FLAG_COMPUTE) == 0)
    def _():
        o_ref[...] = jnp.zeros_like(o_ref)


def _work_tables(counts, rows, nb_total, nj):
    nblk = (counts + rows - 1) // rows
    e_num = nblk.shape[0]
    blk_end = jnp.cumsum(nblk)
    blk_start = blk_end - nblk
    tot = blk_end[-1]
    w = jnp.arange(nj * nb_total, dtype=jnp.int32)
    e_w = jnp.sum(w[:, None] >= (nj * blk_end)[None, :], axis=1).astype(jnp.int32)
    tail = e_w >= e_num
    e_c = jnp.minimum(e_w, e_num - 1)
    nb_e = jnp.maximum(nblk[e_c], 1)
    local = w - nj * blk_start[e_c]
    j = local // nb_e
    r = local - j * nb_e
    ntail = jnp.maximum(nb_total - tot, 1)
    lt = w - nj * tot
    jt = lt // ntail
    rt = lt - jt * ntail
    e_last = e_c[jnp.maximum(nj * tot - 1, 0)]
    blk = jnp.where(tail, tot + rt, blk_start[e_c] + r)
    we = jnp.where(tail, e_last, e_c)
    wj = jnp.where(tail, nj - 1, j)
    oj = jnp.where(tail, jt, j)
    half = (counts[e_c] - r * rows) <= rows // 2
    fl = jnp.where(tail, 0, _FLAG_COMPUTE + jnp.where(r == 0, _FLAG_CAST, 0) + jnp.where(half, _FLAG_HALF, 0))
    i32 = lambda a: a.astype(jnp.int32)
    return i32(blk), i32(we), i32(wj), i32(oj), i32(fl)


def _moe(xn, logits, w_gate_up, b_gate_up, w_down, b_down):
    nt, d = xn.shape
    e_num, _, de2 = w_gate_up.shape
    de = de2 // 2
    rows = MOE_ROWS
    na = nt * TOP_K
    top_v, top_e = lax.top_k(logits, TOP_K)
    gate = jax.nn.softmax(top_v, axis=-1)
    flat_e = top_e.reshape(-1).astype(jnp.int32)
    order = jnp.argsort(flat_e).astype(jnp.int32)
    rank = jnp.argsort(order).astype(jnp.int32)
    counts = jnp.sum(flat_e[:, None] == jnp.arange(e_num, dtype=jnp.int32)[None, :], axis=0).astype(jnp.int32)
    nblk = (counts + rows - 1) // rows
    blk_end = jnp.cumsum(nblk)
    blk_start = blk_end - nblk
    sort_start = jnp.cumsum(counts) - counts
    nb_total = -(-na // rows) + e_num
    n_slots = nb_total * rows
    slot_of = blk_start[flat_e] * rows + rank - sort_start[flat_e]
    s = jnp.arange(n_slots, dtype=jnp.int32)
    e_s = jnp.minimum(jnp.sum((s // rows)[:, None] >= blk_end[None, :], axis=1), e_num - 1).astype(jnp.int32)
    pos = s - blk_start[e_s] * rows
    src = jnp.clip(sort_start[e_s] + pos, 0, na - 1)
    slot_tok = jnp.where(pos < counts[e_s], order[src] // TOP_K, s % nt)
    x_sorted = jnp.take(xn, slot_tok, axis=0, mode="clip")

    tn = _tile(de, 512, _LANES)
    nj = de // tn
    tabs = _work_tables(counts, rows, nb_total, nj)
    bgu = b_gate_up.reshape(e_num, 1, de2)
    act = pl.pallas_call(
        _moe_up_kernel,
        grid_spec=pltpu.PrefetchScalarGridSpec(
            num_scalar_prefetch=5,
            grid=(nj * nb_total,),
            in_specs=[
                pl.BlockSpec((rows, d), lambda w, blk, we, wj, oj, fl: (blk[w], 0)),
                pl.BlockSpec((None, d, tn), lambda w, blk, we, wj, oj, fl: (we[w], 0, wj[w])),
                pl.BlockSpec((None, d, tn), lambda w, blk, we, wj, oj, fl: (we[w], 0, nj + wj[w])),
                pl.BlockSpec((None, 1, tn), lambda w, blk, we, wj, oj, fl: (we[w], 0, wj[w])),
                pl.BlockSpec((None, 1, tn), lambda w, blk, we, wj, oj, fl: (we[w], 0, nj + wj[w])),
            ],
            out_specs=pl.BlockSpec((rows, tn), lambda w, blk, we, wj, oj, fl: (blk[w], oj[w])),
            scratch_shapes=[pltpu.VMEM((d, tn), _BF16), pltpu.VMEM((d, tn), _BF16)],
        ),
        out_shape=jax.ShapeDtypeStruct((n_slots, de), _BF16),
        compiler_params=_params("arbitrary"),
        name="moe_gate_up",
    )(*tabs, x_sorted, w_gate_up, w_gate_up, bgu, bgu)

    tn2 = _tile(d, 1024, _LANES)
    nj2 = d // tn2
    tabs2 = _work_tables(counts, rows, nb_total, nj2)
    y_sorted = pl.pallas_call(
        _moe_down_kernel,
        grid_spec=pltpu.PrefetchScalarGridSpec(
            num_scalar_prefetch=5,
            grid=(nj2 * nb_total,),
            in_specs=[
                pl.BlockSpec((rows, de), lambda w, blk, we, wj, oj, fl: (blk[w], 0)),
                pl.BlockSpec((None, de, tn2), lambda w, blk, we, wj, oj, fl: (we[w], 0, wj[w])),
                pl.BlockSpec((None, 1, tn2), lambda w, blk, we, wj, oj, fl: (we[w], 0, wj[w])),
            ],
            out_specs=pl.BlockSpec((rows, tn2), lambda w, blk, we, wj, oj, fl: (blk[w], oj[w])),
            scratch_shapes=[pltpu.VMEM((de, tn2), _BF16)],
        ),
        out_shape=jax.ShapeDtypeStruct((n_slots, d), _F32),
        compiler_params=_params("arbitrary"),
        name="moe_down",
    )(*tabs2, act, w_down, b_down.reshape(e_num, 1, d))

    idx = slot_of.reshape(nt, TOP_K).T.reshape(-1)
    yk = jnp.take(y_sorted, idx, axis=0, mode="clip").reshape(TOP_K, nt, d)
    return yk, gate


def kernel(x_prompt, x_sample, state_conv, state_mlstm_C, state_mlstm_n, state_mlstm_m, state_hgrn_S,
           g_mix, w_in, b_in, conv_w, conv_b, gain_a, gain_b, lb_logits, w_branch_a, w_branch_b, w_out,
           g_ffn, w_router, b_router, w_gate_up, b_gate_up, w_down, b_down, g_final):
    depth = w_in.shape[0]
    assert depth == 1, "single-layer step"
    bp, tp, d = x_prompt.shape
    bs, ts, _ = x_sample.shape
    nh_a, dk_a, dv_a = state_mlstm_C.shape[2:]
    nh_b, dh_b = state_hgrn_S.shape[2:4]
    qk2, va, wb = 2 * nh_a * dk_a, nh_a * dv_a, nh_b * dh_b
    assert qk2 == d and va == d and wb == d, "column blocks of the projection must have equal width"
    np_rows, ns_rows = bp * tp, bs * ts
    l = 0

    x = jnp.concatenate([x_prompt.reshape(np_rows, d), x_sample.reshape(ns_rows, d)], axis=0)

    sizes = (qk2, va, va, nh_a, nh_a, wb, wb, wb, wb, d, d)
    offs = np.concatenate([[0], np.cumsum(sizes)])
    big = [i for i, s in enumerate(sizes) if s == d]
    wt_l, b_l = jnp.swapaxes(w_in[l], 0, 1), b_in[l]
    wt_main = _pack_rows(wt_l, int(offs[3]), int(offs[5]), min(d, 512))
    b_main = jnp.concatenate([b_l[offs[i]:offs[i + 1]] for i in big])
    ng = 2 * nh_a
    wt_gate = jnp.pad(wt_l[offs[3]:offs[5], :], ((0, _LANES - ng), (0, 0))).astype(_BF16)
    b_gate = jnp.pad(b_l[offs[3]:offs[5]], (0, _LANES - ng))

    u = _rms_cast(x, g_mix[l], _BF16)
    proj = _mm_bias_nt(u, wt_main, b_main, _F32, tm_pref=832)
    gates = _mm_bias_nt(u, wt_gate, b_gate, _F32, tn_pref=_LANES)[:, :ng]

    lb = jnp.cumsum(jax.nn.softmax(lb_logits.astype(_F32), axis=0), axis=0)[l]

    def mixers(row0, B, T, conv0, c0, n0, m0, s0):
        ya, cv, cn, nn, mn = _mlstm(proj, gates, row0, B, T, conv0, c0, n0, m0,
                                    conv_w[l], conv_b[l], gain_a[l], cols=(0, 1, 2))
        yb, sn = _hgrn(proj, row0, B, T, s0, lb, gain_b[l], cols=(3, 4, 5, 6))
        return ya, yb, (cv, cn, nn, mn, sn)

    zeros = lambda *s: jnp.zeros(s, _F32)
    ya_p, yb_p, st_p = mixers(0, bp, tp, zeros(bp, conv_w.shape[1] - 1, qk2), zeros(bp, nh_a, dk_a, dv_a),
                              zeros(bp, nh_a, dk_a), zeros(bp, nh_a), zeros(bp, nh_b, dh_b, dh_b))
    ya_s, yb_s, st_s = mixers(np_rows, bs, ts, state_conv[l], state_mlstm_C[l], state_mlstm_n[l],
                              state_mlstm_m[l], state_hgrn_S[l])
    ya = jnp.concatenate([ya_p, ya_s], axis=0)
    yb = jnp.concatenate([yb_p, yb_s], axis=0)

    z = _merge(ya, yb, w_branch_a[l].astype(_BF16), w_branch_b[l].astype(_BF16), proj, 7, 8)
    h = _outproj(z, w_out[l].astype(_BF16), x)

    xn, logits = _norm_router(h, g_ffn[l], w_router[l], b_router[l])
    yk, gate = _moe(xn, logits, w_gate_up[l], b_gate_up[l], w_down[l], b_down[l])
    out_p, out_s = _final_norm(h, yk, gate, g_final, np_rows, ns_rows)

    states = [s[None] for s in st_p] + [s[None] for s in st_s]
    return (out_p.reshape(bp, tp, d), out_s.reshape(bs, ts, d), *states)
```
